```python
import math
import jax, jax.numpy as jnp
from jax import lax
import numpy as np

D_MODEL = 1024
BATCH = 4
SEQ = 8192
DEPTH = 1
DEC_BATCH = 32
DEC_SEQ = 4
PAST_LEN = 16384
PAGE_SIZE = 128

N_HEADS = 8
HEAD_DIM = D_MODEL // (2 * N_HEADS)
QK_DIM = 2 * N_HEADS * HEAD_DIM
V_DIM = 2 * HEAD_DIM
ATTN_DIM = N_HEADS * V_DIM
ROPE_THETA = 10000.0
Q_BLOCK = 128
CONV_DIM = D_MODEL
CONV_WIDTH = 3
N_MEM = 256
MEM_HEADS = 4
MEM_HEAD_DIM = D_MODEL // MEM_HEADS
N_GROUPS = 4
EXPERTS_PER_GROUP = 8
N_EXPERTS = N_GROUPS * EXPERTS_PER_GROUP
EXPERT_TOP_K = 2
D_EXPERT = D_MODEL // 4
MOE_BLOCK = 128
EPS = 1e-6
SUBLN_EPS = 1e-5

_IN_SIZES = (QK_DIM, QK_DIM, ATTN_DIM, CONV_DIM, CONV_DIM, CONV_DIM, D_MODEL, D_MODEL)
IN_COLS = sum(_IN_SIZES)
SPLIT_POINTS = tuple(int(v) for v in np.cumsum(_IN_SIZES)[:-1])

kernel_name = 'diffattn_shortconv_hmoe_decoder_step'


def rmsnorm(x, g, eps=EPS):
    xf = x.astype(jnp.float32)
    y = xf * lax.rsqrt(jnp.mean(xf * xf, axis=-1, keepdims=True) + eps)
    return (y * g.astype(jnp.float32)).astype(x.dtype)


def rope(x, pos):
    half = x.shape[-1] // 2
    inv = 1.0 / (ROPE_THETA ** (jnp.arange(half, dtype=jnp.float32) / half))
    ang = pos.astype(jnp.float32)[:, None] * inv[None, :]
    cos = jnp.cos(ang)[None, :, None, :]
    sin = jnp.sin(ang)[None, :, None, :]
    xf = x.astype(jnp.float32)
    x1, x2 = xf[..., :half], xf[..., half:]
    return jnp.concatenate([x1 * cos - x2 * sin, x2 * cos + x1 * sin], axis=-1).astype(x.dtype)


def lambda_init(layer_idx):
    return 0.8 - 0.6 * math.exp(-0.3 * layer_idx)


def diff_lambda(lq1, lk1, lq2, lk2, lam_init):
    f = lambda a, b: jnp.exp(jnp.sum(a.astype(jnp.float32) * b.astype(jnp.float32)))
    return f(lq1, lk1) - f(lq2, lk2) + lam_init


def mixer_inputs(x, pos, ln_g, w_in):
    b, t, _ = x.shape
    z = rmsnorm(x, ln_g) @ w_in
    q, k, v, b_gate, c_gate, xc, g_attn, g_conv = jnp.split(z, SPLIT_POINTS, axis=-1)
    q = rope(q.reshape(b, t, 2 * N_HEADS, HEAD_DIM), pos)
    k = rope(k.reshape(b, t, 2 * N_HEADS, HEAD_DIM), pos)
    v = v.reshape(b, t, N_HEADS, V_DIM)
    u = c_gate * xc
    return q, k, v, u, b_gate, g_attn, g_conv


def diff_attention(q, segments, q_pos, lam):
    scale = HEAD_DIM ** -0.5
    scores = []
    for k, _, k_pos in segments:
        s = jnp.einsum('bqhd,bkhd->bhqk', q, k, preferred_element_type=jnp.float32) * scale
        mask = k_pos[None, :] <= q_pos[:, None]
        scores.append(jnp.where(mask[None, None], s, -jnp.inf))
    p = jax.nn.softmax(jnp.concatenate(scores, axis=-1), axis=-1)
    b, _, tq, tk = p.shape
    p = p.reshape(b, N_HEADS, 2, tq, tk)
    a = p[:, :, 0] - lam * p[:, :, 1]
    out = None
    off = 0
    for _, v, k_pos in segments:
        n = k_pos.shape[0]
        o = jnp.einsum('bhqk,bkhe->bqhe', a[..., off:off + n].astype(v.dtype), v)
        out = o if out is None else out + o
        off += n
    return out


def prompt_diff_attention(q, k, v, pos, lam):
    b, t = q.shape[:2]
    nb = t // Q_BLOCK
    qb = q.reshape(b, nb, Q_BLOCK, 2 * N_HEADS, HEAD_DIM).swapaxes(0, 1)
    pb = pos.reshape(nb, Q_BLOCK)
    ob = lax.map(lambda a: diff_attention(a[0], ((k, v, pos),), a[1], lam), (qb, pb))
    return ob.swapaxes(0, 1).reshape(b, t, N_HEADS, V_DIM)


def head_norm(o, subln_g, lam_init):
    b, t = o.shape[:2]
    o = rmsnorm(o, subln_g, SUBLN_EPS) * (1.0 - lam_init)
    return o.reshape(b, t, ATTN_DIM)


def causal_conv(u_pad, w):
    t = u_pad.shape[1] - (CONV_WIDTH - 1)
    y = u_pad[:, 0:t] * w[0]
    for j in range(1, CONV_WIDTH):
        y = y + u_pad[:, j:j + t] * w[j]
    return y


def merge_branches(x, o_attn, y_conv, b_gate, g_attn, g_conv, w_proj_attn, w_proj_conv, w_out):
    branch_a = o_attn @ w_proj_attn
    branch_c = (b_gate * y_conv) @ w_proj_conv
    mixed = jax.nn.sigmoid(g_attn) * branch_a + jax.nn.sigmoid(g_conv) * branch_c
    return x + mixed @ w_out


def memory_kv(mem, ln_g, w_mk, w_mv):
    b, m, _ = mem.shape
    h = rmsnorm(mem, ln_g)
    k = (h @ w_mk).reshape(b, m, MEM_HEADS, MEM_HEAD_DIM)
    v = (h @ w_mv).reshape(b, m, MEM_HEADS, MEM_HEAD_DIM)
    return k, v


def memory_attention(x, mk, mv, ln_g, w_mq, w_mo):
    b, t, _ = x.shape
    q = (rmsnorm(x, ln_g) @ w_mq).reshape(b, t, MEM_HEADS, MEM_HEAD_DIM)
    s = jnp.einsum('bqhd,bkhd->bhqk', q, mk, preferred_element_type=jnp.float32) * (MEM_HEAD_DIM ** -0.5)
    p = jax.nn.softmax(s, axis=-1).astype(mv.dtype)
    o = jnp.einsum('bhqk,bkhd->bqhd', p, mv).reshape(b, t, D_MODEL)
    return x + o @ w_mo


def moe_ffn(h, w_group, b_group, w_router, b_router, w1, w3, w2):
    t = h.shape[0]
    g_logits = jnp.dot(h, w_group).astype(jnp.float32) + b_group.astype(jnp.float32)
    g_idx = jnp.argmax(g_logits, axis=-1)
    g_prob = jnp.take_along_axis(jax.nn.softmax(g_logits, axis=-1), g_idx[:, None], axis=-1)[:, 0]
    e_logits = (jnp.dot(h, w_router).astype(jnp.float32) + b_router.astype(jnp.float32)).reshape(t, N_GROUPS, EXPERTS_PER_GROUP)
    e_logits = jnp.take_along_axis(e_logits, g_idx[:, None, None], axis=1)[:, 0]
    top_l, top_i = lax.top_k(e_logits, EXPERT_TOP_K)
    top_p = jax.nn.softmax(top_l, axis=-1)
    gate = (g_prob[:, None] * top_p).astype(h.dtype).reshape(-1)
    eid = (g_idx[:, None] * EXPERTS_PER_GROUP + top_i).reshape(-1).astype(jnp.int32)
    tid = jnp.repeat(jnp.arange(t, dtype=jnp.int32), EXPERT_TOP_K)
    n_assign = t * EXPERT_TOP_K
    order = jnp.argsort(eid)
    eid_s, tid_s, gate_s = eid[order], tid[order], gate[order]
    counts = jnp.zeros((N_EXPERTS,), jnp.int32).at[eid].add(1)
    starts = jnp.cumsum(counts) - counts
    padded = (counts + MOE_BLOCK - 1) // MOE_BLOCK * MOE_BLOCK
    pends = jnp.cumsum(padded)
    pstarts = pends - padded
    slot = pstarts[eid_s] + jnp.arange(n_assign, dtype=jnp.int32) - starts[eid_s]
    n_blocks = -(-n_assign // MOE_BLOCK) + N_EXPERTS
    n_slots = n_blocks * MOE_BLOCK
    slot_tok = jnp.zeros((n_slots,), jnp.int32).at[slot].set(tid_s)
    slot_gate = jnp.zeros((n_slots,), h.dtype).at[slot].set(gate_s)
    blk_start = jnp.arange(n_blocks, dtype=jnp.int32) * MOE_BLOCK
    blk_e = jnp.minimum(jnp.searchsorted(pends, blk_start, side='right'), N_EXPERTS - 1).astype(jnp.int32)
    xs = h[slot_tok].reshape(n_blocks, MOE_BLOCK, h.shape[-1])

    def expert_block(args):
        xb, e = args
        return (jax.nn.silu(xb @ w1[e]) * (xb @ w3[e])) @ w2[e]

    ys = lax.map(expert_block, (xs, blk_e)).reshape(n_slots, h.shape[-1])
    return jnp.zeros_like(h).at[slot_tok].add(ys * slot_gate[:, None])


def moe_sublayer(x, ln_g, w_group, b_group, w_router, b_router, w1, w3, w2):
    b, t, d = x.shape
    h = rmsnorm(x, ln_g).reshape(b * t, d)
    return x + moe_ffn(h, w_group, b_group, w_router, b_router, w1, w3, w2).reshape(b, t, d)


def setup_inputs(seed: int = 0) -> dict:
    key = jax.random.key(seed)
    ks = iter(jax.random.split(key, 48))
    nrm = lambda shape, scale: jax.random.normal(next(ks), shape, jnp.float32) * scale
    gain = lambda shape: 1.0 + nrm(shape, 0.02)
    n_pages = PAST_LEN // PAGE_SIZE
    n_used = DEC_BATCH * n_pages
    n_pool = n_used + (n_used + 3) // 4
    ds = D_MODEL ** -0.5
    x_prompt = nrm((BATCH, SEQ, D_MODEL), 1.0)
    x_sample = nrm((DEC_BATCH, DEC_SEQ, D_MODEL), 1.0)
    cache_k = nrm((DEPTH, n_pool, PAGE_SIZE, 2 * N_HEADS, HEAD_DIM), 1.0)
    cache_v = nrm((DEPTH, n_pool, PAGE_SIZE, N_HEADS, V_DIM), 1.0)
    state_conv = nrm((DEPTH, DEC_BATCH, CONV_WIDTH - 1, CONV_DIM), 1.0)
    cache_mem_k = nrm((DEPTH, DEC_BATCH, N_MEM, MEM_HEADS, MEM_HEAD_DIM), 1.0)
    cache_mem_v = nrm((DEPTH, DEC_BATCH, N_MEM, MEM_HEADS, MEM_HEAD_DIM), 1.0)
    page_table = jax.random.permutation(next(ks), n_pool)[:n_used].reshape(DEC_BATCH, n_pages).astype(jnp.int32)
    mem_prompt = nrm((BATCH, N_MEM, D_MODEL), 1.0)
    return {
        'x_prompt': x_prompt,
        'x_sample': x_sample,
        'cache_k': cache_k,
        'cache_v': cache_v,
        'state_conv': state_conv,
        'cache_mem_k': cache_mem_k,
        'cache_mem_v': cache_mem_v,
        'page_table': page_table,
        'mem_prompt': mem_prompt,
        'ln_mix_g': gain((DEPTH, D_MODEL)),
        'w_in': nrm((DEPTH, D_MODEL, IN_COLS), ds),
        'lambda_q1': nrm((DEPTH, HEAD_DIM), 0.1),
        'lambda_k1': nrm((DEPTH, HEAD_DIM), 0.1),
        'lambda_q2': nrm((DEPTH, HEAD_DIM), 0.1),
        'lambda_k2': nrm((DEPTH, HEAD_DIM), 0.1),
        'subln_g': gain((DEPTH, V_DIM)),
        'conv_w': nrm((DEPTH, CONV_WIDTH, CONV_DIM), CONV_WIDTH ** -0.5),
        'w_proj_attn': nrm((DEPTH, ATTN_DIM, D_MODEL), ATTN_DIM ** -0.5),
        'w_proj_conv': nrm((DEPTH, CONV_DIM, D_MODEL), CONV_DIM ** -0.5),
        'w_out': nrm((DEPTH, D_MODEL, D_MODEL), ds),
        'ln_mem_g': gain((DEPTH, D_MODEL)),
        'ln_memkv_g': gain((DEPTH, D_MODEL)),
        'w_mq': nrm((DEPTH, D_MODEL, D_MODEL), ds),
        'w_mk': nrm((DEPTH, D_MODEL, D_MODEL), ds),
        'w_mv': nrm((DEPTH, D_MODEL, D_MODEL), ds),
        'w_mo': nrm((DEPTH, D_MODEL, D_MODEL), ds),
        'ln_ffn_g': gain((DEPTH, D_MODEL)),
        'w_group': nrm((DEPTH, D_MODEL, N_GROUPS), ds),
        'b_group': nrm((DEPTH, N_GROUPS), 0.01),
        'w_router': nrm((DEPTH, D_MODEL, N_EXPERTS), ds),
        'b_router': nrm((DEPTH, N_EXPERTS), 0.01),
        'w1': nrm((DEPTH, N_EXPERTS, D_MODEL, D_EXPERT), ds),
        'w3': nrm((DEPTH, N_EXPERTS, D_MODEL, D_EXPERT), ds),
        'w2': nrm((DEPTH, N_EXPERTS, D_EXPERT, D_MODEL), D_EXPERT ** -0.5),
        'ln_final_g': gain((D_MODEL,)),
    }


def reference(x_prompt, x_sample, cache_k, cache_v, state_conv, cache_mem_k, cache_mem_v, page_table, mem_prompt,
              ln_mix_g, w_in, lambda_q1, lambda_k1, lambda_q2, lambda_k2, subln_g, conv_w, w_proj_attn, w_proj_conv, w_out,
              ln_mem_g, ln_memkv_g, w_mq, w_mk, w_mv, w_mo, ln_ffn_g, w_group, b_group, w_router, b_router,
              w1, w3, w2, ln_final_g):
    b, s, _ = x_prompt.shape
    db, dsq, _ = x_sample.shape
    past = page_table.shape[1] * cache_k.shape[2]
    pos_p = jnp.arange(s, dtype=jnp.int32)
    pos_past = jnp.arange(past, dtype=jnp.int32)
    pos_s = past + jnp.arange(dsq, dtype=jnp.int32)
    xp, xs = x_prompt, x_sample
    nk_p, nv_p, nc_p, nmk_p, nmv_p = [], [], [], [], []
    nk_s, nv_s, nc_s = [], [], []
    for l in range(DEPTH):
        lam_init = lambda_init(l)
        lam = diff_lambda(lambda_q1[l], lambda_k1[l], lambda_q2[l], lambda_k2[l], lam_init)
        ffn_w = (ln_ffn_g[l], w_group[l], b_group[l], w_router[l], b_router[l], w1[l], w3[l], w2[l])

        q, k, v, u, b_gate, g_attn, g_conv = mixer_inputs(xp, pos_p, ln_mix_g[l], w_in[l])
        o = head_norm(prompt_diff_attention(q, k, v, pos_p, lam), subln_g[l], lam_init)
        u_pad = jnp.concatenate([jnp.zeros((b, CONV_WIDTH - 1, CONV_DIM), u.dtype), u], axis=1)
        y_conv = causal_conv(u_pad, conv_w[l])
        xp = merge_branches(xp, o, y_conv, b_gate, g_attn, g_conv, w_proj_attn[l], w_proj_conv[l], w_out[l])
        mk, mv = memory_kv(mem_prompt, ln_memkv_g[l], w_mk[l], w_mv[l])
        xp = memory_attention(xp, mk, mv, ln_mem_g[l], w_mq[l], w_mo[l])
        xp = moe_sublayer(xp, *ffn_w)
        nk_p.append(k)
        nv_p.append(v)
        nc_p.append(u_pad[:, -(CONV_WIDTH - 1):])
        nmk_p.append(mk)
        nmv_p.append(mv)

        q, k, v, u, b_gate, g_attn, g_conv = mixer_inputs(xs, pos_s, ln_mix_g[l], w_in[l])
        k_past = cache_k[l, page_table].reshape(db, past, 2 * N_HEADS, HEAD_DIM)
        v_past = cache_v[l, page_table].reshape(db, past, N_HEADS, V_DIM)
        o = diff_attention(q, ((k_past, v_past, pos_past), (k, v, pos_s)), pos_s, lam)
        o = head_norm(o, subln_g[l], lam_init)
        u_pad = jnp.concatenate([state_conv[l].astype(u.dtype), u], axis=1)
        y_conv = causal_conv(u_pad, conv_w[l])
        xs = merge_branches(xs, o, y_conv, b_gate, g_attn, g_conv, w_proj_attn[l], w_proj_conv[l], w_out[l])
        xs = memory_attention(xs, cache_mem_k[l], cache_mem_v[l], ln_mem_g[l], w_mq[l], w_mo[l])
        xs = moe_sublayer(xs, *ffn_w)
        nk_s.append(k)
        nv_s.append(v)
        nc_s.append(u_pad[:, -(CONV_WIDTH - 1):])

    y_prompt = rmsnorm(xp, ln_final_g)
    y_sample = rmsnorm(xs, ln_final_g)
    return (y_prompt, y_sample,
            jnp.stack(nk_p), jnp.stack(nv_p), jnp.stack(nc_p), jnp.stack(nmk_p), jnp.stack(nmv_p),
            jnp.stack(nk_s), jnp.stack(nv_s), jnp.stack(nc_s))
```

```python
import functools
import math

import jax
import jax.numpy as jnp
from jax import lax
from jax.experimental import pallas as pl
from jax.experimental.pallas import tpu as pltpu

N_HEADS = 8
HEAD_DIM = 64
V_DIM = 2 * HEAD_DIM
ROPE_THETA = 10000.0
CONV_WIDTH = 3
MEM_HEADS = 4
N_GROUPS = 4
EXPERTS_PER_GROUP = 8
N_EXPERTS = N_GROUPS * EXPERTS_PER_GROUP
EXPERT_TOP_K = 2
MOE_BLOCK = 128
EPS = 1e-6
SUBLN_EPS = 1e-5

LANES = 128
SUBLANES = 8
VMEM_LIMIT_BYTES = 56 * 1024 * 1024

MIX_ROWS = 256
ATTN_TILE = 512
PAGES_PER_STEP = 8
SAMPLE_ROWS = 8
COMBINE_ROWS = 128

NEG = -1e30
F32 = jnp.float32
BF16 = jnp.bfloat16


def _params(sem):
    return pltpu.CompilerParams(dimension_semantics=sem, vmem_limit_bytes=VMEM_LIMIT_BYTES)


def _rms(x, g, eps):
    return x * lax.rsqrt(jnp.mean(x * x, axis=-1, keepdims=True) + eps) * g


def _dot(a, b):
    return jnp.dot(a, b, preferred_element_type=F32)


def _dot_nt(a, b):
    return lax.dot_general(a, b, (((1,), (1,)), ((), ())), preferred_element_type=F32)


def _const_spec(shape):
    nd = len(shape)
    return pl.BlockSpec(shape, lambda *_: (0,) * nd, pipeline_mode=pl.Buffered(1))


def _lambda_value(lamv, lam_init):
    a = jnp.sum(lamv[0:1, :] * lamv[1:2, :], axis=1, keepdims=True)
    b = jnp.sum(lamv[2:3, :] * lamv[3:4, :], axis=1, keepdims=True)
    return jnp.exp(a) - jnp.exp(b) + lam_init


def _head_norm(o, g, lam_init):
    return _rms(o, g, SUBLN_EPS) * (1.0 - lam_init)


def _mixer_in_kernel(*refs, tm, d, seq_tiles, period):
    if period:
        (x_ref, g_ref, w_ref, cos_ref, sin_ref, cw_ref, s1_ref, s2_ref,
         q_ref, kf_ref, kb_ref, vf_ref, vb_ref, bc_ref, sa_ref, sc_ref, tail_ref, ubuf_ref) = refs
    else:
        (x_ref, g_ref, w_ref, cos_ref, sin_ref, cw_ref,
         q_ref, kf_ref, kb_ref, vf_ref, vb_ref, bc_ref, sa_ref, sc_ref, tail_ref, ubuf_ref) = refs
    i = pl.program_id(0)
    h = _rms(x_ref[...], g_ref[...], EPS).astype(BF16)

    def proj(j):
        return _dot(h, w_ref[:, j * d:(j + 1) * d])

    cos = cos_ref[...]
    sin = sin_ref[...]
    lane = lax.broadcasted_iota(jnp.int32, (tm, LANES), 1)
    first_half = (lane % HEAD_DIM) < (HEAD_DIM // 2)

    def rope_chunks(z):
        for c in range(d // LANES):
            zc = z[:, c * LANES:(c + 1) * LANES]
            zr = jnp.where(first_half, pltpu.roll(zc, LANES - HEAD_DIM // 2, 1), pltpu.roll(zc, HEAD_DIM // 2, 1))
            yield c, zc * cos + zr * sin

    zq = proj(0)
    for c, y in rope_chunks(zq):
        q_ref[:, c * LANES:(c + 1) * LANES] = (y * (HEAD_DIM ** -0.5)).astype(BF16)
    zk = proj(1)
    for c, y in rope_chunks(zk):
        kf_ref[:, c * LANES:(c + 1) * LANES] = y
        kb_ref[:, c * LANES:(c + 1) * LANES] = y.astype(BF16)
    zv = proj(2)
    vf_ref[...] = zv
    vb_ref[...] = zv.astype(BF16)

    b_gate = proj(3)
    u = proj(4) * proj(5)
    @pl.when(i % max(seq_tiles, 1) == 0)
    def _():
        ubuf_ref[0:SUBLANES, :] = jnp.zeros((SUBLANES, d), F32)
    ubuf_ref[SUBLANES:SUBLANES + tm, :] = u
    p1 = ubuf_ref[SUBLANES - 1:SUBLANES - 1 + tm, :]
    p2 = ubuf_ref[SUBLANES - 2:SUBLANES - 2 + tm, :]
    if period:
        r = lax.broadcasted_iota(jnp.int32, (tm, 1), 0) % period
        p1 = jnp.where(r == 0, s1_ref[...], p1)
        p2 = jnp.where(r < 2, s2_ref[...], p2)
    cw = cw_ref[...]
    y_conv = p2 * cw[0:1, :] + p1 * cw[1:2, :] + u * cw[2:3, :]
    bc_ref[...] = (b_gate * y_conv).astype(BF16)
    if seq_tiles:
        last = ubuf_ref[tm:tm + SUBLANES, :]
        ubuf_ref[0:SUBLANES, :] = last
        tail_ref[0] = last
    else:
        tail_ref[...] = u

    sa_ref[...] = jax.nn.sigmoid(proj(6)).astype(BF16)
    sc_ref[...] = jax.nn.sigmoid(proj(7)).astype(BF16)


def _mixer_in(x2d, ln_g, w_in_b, cos_t, sin_t, conv_w, *, seq_len, state_rows=None):
    t, d = x2d.shape
    tm = min(MIX_ROWS, t)
    assert t % tm == 0
    n_tiles = t // tm
    if state_rows is None:
        assert seq_len % tm == 0
        seq_tiles, period = seq_len // tm, 0
        n_seq = t // seq_len
        pos_tiles = seq_tiles
    else:
        assert tm % seq_len == 0 and n_tiles == 1
        seq_tiles, period = 0, seq_len
        pos_tiles = 1
    row = lambda i: (i, 0)
    pos = lambda i: (i % pos_tiles, 0)
    in_specs = [
        pl.BlockSpec((tm, d), row),
        _const_spec((1, d)),
        _const_spec(w_in_b.shape),
        pl.BlockSpec((tm, LANES), pos),
        pl.BlockSpec((tm, LANES), pos),
        _const_spec((SUBLANES, d)),
    ]
    args = [x2d, ln_g.reshape(1, d), w_in_b, cos_t, sin_t, jnp.pad(conv_w, ((0, SUBLANES - CONV_WIDTH), (0, 0)))]
    if period:
        in_specs += [pl.BlockSpec((tm, d), row), pl.BlockSpec((tm, d), row)]
        args += list(state_rows)
        tail_shape = jax.ShapeDtypeStruct((t, d), F32)
        tail_spec = pl.BlockSpec((tm, d), row)
    else:
        tail_shape = jax.ShapeDtypeStruct((n_seq, SUBLANES, d), F32)
        tail_spec = pl.BlockSpec((1, SUBLANES, d), lambda i: (i // seq_tiles, 0, 0))
    bspec = pl.BlockSpec((tm, d), row)
    sd = lambda dt: jax.ShapeDtypeStruct((t, d), dt)
    return pl.pallas_call(
        functools.partial(_mixer_in_kernel, tm=tm, d=d, seq_tiles=seq_tiles, period=period),
        grid=(n_tiles,),
        in_specs=in_specs,
        out_specs=[bspec] * 8 + [tail_spec],
        out_shape=[sd(BF16), sd(F32), sd(BF16), sd(F32), sd(BF16), sd(BF16), sd(BF16), sd(BF16), tail_shape],
        scratch_shapes=[pltpu.VMEM((tm + 2 * SUBLANES, d), F32)],
        compiler_params=_params(("arbitrary",)),
        name="mixer_in",
    )(*args)


def _prompt_attn_kernel(lamv_ref, g_ref, q_ref, k_ref, v_ref, o_ref, m_ref, l_ref, acc_ref, *, tile, lam_init):
    qi = pl.program_id(2)
    q = q_ref[0]
    lane = lax.broadcasted_iota(jnp.int32, q.shape, 1)
    zero = jnp.zeros_like(q)
    qs = (jnp.where(lane < HEAD_DIM, q, zero), jnp.where(lane >= HEAD_DIM, q, zero))
    m_ref[...] = jnp.full(m_ref.shape, NEG, F32)
    l_ref[...] = jnp.zeros(l_ref.shape, F32)
    acc_ref[...] = jnp.zeros(acc_ref.shape, F32)
    row = lax.broadcasted_iota(jnp.int32, (tile, tile), 0)
    col = lax.broadcasted_iota(jnp.int32, (tile, tile), 1)

    def step(ki, diagonal):
        off = pl.multiple_of(ki * tile, tile)
        k = k_ref[0, pl.ds(off, tile), :]
        v = v_ref[0, pl.ds(off, tile), :]
        for j in range(2):
            s = _dot_nt(qs[j], k)
            if diagonal:
                s = jnp.where(col <= row, s, NEG)
            m_old = m_ref[j]
            m_new = jnp.maximum(m_old, jnp.max(s, axis=1, keepdims=True))
            alpha = jnp.exp(m_old - m_new)
            p = jnp.exp(s - m_new)
            l_ref[j] = alpha * l_ref[j] + jnp.sum(p, axis=1, keepdims=True)
            acc_ref[j] = alpha * acc_ref[j] + _dot(p.astype(BF16), v)
            m_ref[j] = m_new

    def body(ki, carry):
        step(ki, False)
        return carry

    lax.fori_loop(0, qi, body, 0)
    step(qi, True)

    lam = _lambda_value(lamv_ref[...], lam_init)
    o = acc_ref[0] / l_ref[0] - lam * (acc_ref[1] / l_ref[1])
    o_ref[0] = _head_norm(o, g_ref[...], lam_init).astype(BF16)


def _prompt_attention(q, k, v, lamv, subln_g, lam_init):
    b, s, d = q.shape
    tile = min(ATTN_TILE, s)
    assert s % tile == 0
    nq = s // tile
    return pl.pallas_call(
        functools.partial(_prompt_attn_kernel, tile=tile, lam_init=lam_init),
        grid=(b, N_HEADS, nq),
        in_specs=[
            _const_spec((SUBLANES, LANES)),
            _const_spec((1, V_DIM)),
            pl.BlockSpec((1, tile, LANES), lambda bi, h, qi: (bi, qi, h)),
            pl.BlockSpec((1, s, LANES), lambda bi, h, qi: (bi, 0, h)),
            pl.BlockSpec((1, s, LANES), lambda bi, h, qi: (bi, 0, h)),
        ],
        out_specs=pl.BlockSpec((1, tile, LANES), lambda bi, h, qi: (bi, qi, h)),
        out_shape=jax.ShapeDtypeStruct((b, s, d), BF16),
        scratch_shapes=[
            pltpu.VMEM((2, tile, 1), F32),
            pltpu.VMEM((2, tile, 1), F32),
            pltpu.VMEM((2, tile, V_DIM), F32),
        ],
        compiler_params=_params(("arbitrary", "arbitrary", "arbitrary")),
        name="prompt_attention",
    )(lamv, subln_g.reshape(1, V_DIM), q, k, v)


def _decode_attn_kernel(pt_ref, lamv_ref, g_ref, qbd_ref, kn_ref, vn_ref, *refs, n_pages_step, n_new, lam_init):
    k_refs = refs[:n_pages_step]
    v_refs = refs[n_pages_step:2 * n_pages_step]
    o_ref, m_ref, l_ref, acc_ref = refs[2 * n_pages_step:]
    p_id = pl.program_id(1)
    qbd = qbd_ref[0]
    d = qbd.shape[0]
    n_cols = qbd.shape[1]

    def rows_to_cols(x_row):
        return jnp.transpose(jnp.broadcast_to(x_row, (LANES, n_cols)))

    def update(k, v, visible=None):
        s = _dot(k.astype(BF16), qbd)
        if visible is not None:
            s = jnp.where(visible, s, NEG)
        m_old = m_ref[...]
        m_new = jnp.maximum(m_old, jnp.max(s, axis=0, keepdims=True))
        alpha = jnp.exp(m_old - m_new)
        p = jnp.exp(s - m_new)
        l_ref[...] = alpha * l_ref[...] + jnp.sum(p, axis=0, keepdims=True)
        m_ref[...] = m_new
        pv = _dot(jnp.transpose(p).astype(BF16), v.astype(BF16))
        alpha_c = rows_to_cols(alpha)
        for c in range(d // LANES):
            sl = slice(c * LANES, (c + 1) * LANES)
            acc_ref[:, sl] = alpha_c * acc_ref[:, sl] + pv[:, sl]

    @pl.when(p_id == 0)
    def _():
        m_ref[...] = jnp.full(m_ref.shape, NEG, F32)
        l_ref[...] = jnp.zeros(l_ref.shape, F32)
        acc_ref[...] = jnp.zeros(acc_ref.shape, F32)
        shape = (kn_ref.shape[1], n_cols)
        j = lax.broadcasted_iota(jnp.int32, shape, 0)
        r = lax.broadcasted_iota(jnp.int32, shape, 1) % SAMPLE_ROWS
        update(kn_ref[0], vn_ref[0], (j <= r) & (j < n_new))

    update(jnp.concatenate([r[0] for r in k_refs], axis=0), jnp.concatenate([r[0] for r in v_refs], axis=0))

    @pl.when(p_id == pl.num_programs(1) - 1)
    def _():
        lam = _lambda_value(lamv_ref[...], lam_init)
        linv_c = rows_to_cols(1.0 / l_ref[...])
        g = g_ref[...]
        for vh in range(N_HEADS):
            sl = slice(vh * V_DIM, (vh + 1) * V_DIM)
            r1 = slice((2 * vh) * SAMPLE_ROWS, (2 * vh + 1) * SAMPLE_ROWS)
            r2 = slice((2 * vh + 1) * SAMPLE_ROWS, (2 * vh + 2) * SAMPLE_ROWS)
            o1 = acc_ref[r1, sl] * linv_c[r1, :]
            o2 = acc_ref[r2, sl] * linv_c[r2, :]
            o_ref[0, :, sl] = _head_norm(o1 - lam * o2, g, lam_init)


def _decode_attention(page_table, qbd, k_new, v_new, cache_k, cache_v, lamv, subln_g, lam_init, n_new):
    db, d, n_cols = qbd.shape
    n_pages = page_table.shape[1]
    page = cache_k.shape[1]
    nps = math.gcd(PAGES_PER_STEP, n_pages)
    steps = n_pages // nps

    def page_spec(j):
        return pl.BlockSpec((1, page, d), lambda b, p, pt: (pt[b, p * nps + j], 0, 0))

    grid_spec = pltpu.PrefetchScalarGridSpec(
        num_scalar_prefetch=1,
        grid=(db, steps),
        in_specs=[
            pl.BlockSpec((SUBLANES, LANES), lambda b, p, pt: (0, 0)),
            pl.BlockSpec((1, V_DIM), lambda b, p, pt: (0, 0)),
            pl.BlockSpec((1, d, n_cols), lambda b, p, pt: (b, 0, 0)),
            pl.BlockSpec((1, page, d), lambda b, p, pt: (b, 0, 0)),
            pl.BlockSpec((1, page, d), lambda b, p, pt: (b, 0, 0)),
        ] + [page_spec(j) for j in range(nps)] * 2,
        out_specs=pl.BlockSpec((1, SAMPLE_ROWS, d), lambda b, p, pt: (b, 0, 0)),
        scratch_shapes=[
            pltpu.VMEM((1, n_cols), F32),
            pltpu.VMEM((1, n_cols), F32),
            pltpu.VMEM((n_cols, d), F32),
        ],
    )
    return pl.pallas_call(
        functools.partial(_decode_attn_kernel, n_pages_step=nps, n_new=n_new, lam_init=lam_init),
        grid_spec=grid_spec,
        out_shape=jax.ShapeDtypeStruct((db, SAMPLE_ROWS, d), F32),
        compiler_params=_params(("arbitrary", "arbitrary")),
        name="decode_attention",
    )(page_table, lamv, subln_g.reshape(1, V_DIM), qbd, k_new, v_new,
      *([cache_k] * nps), *([cache_v] * nps))


def _memory_kv_kernel(mem_ref, g_ref, wk_ref, wv_ref, kf_ref, vf_ref, kb_ref, vb_ref):
    h = _rms(mem_ref[0], g_ref[...], EPS).astype(BF16)
    k = _dot(h, wk_ref[...])
    v = _dot(h, wv_ref[...])
    kf_ref[0] = k
    vf_ref[0] = v
    kb_ref[0] = k.astype(BF16)
    vb_ref[0] = v.astype(BF16)


def _memory_kv(mem, ln_g, w_mk_b, w_mv_b):
    b, m, d = mem.shape
    spec = pl.BlockSpec((1, m, d), lambda i: (i, 0, 0))
    return pl.pallas_call(
        _memory_kv_kernel,
        grid=(b,),
        in_specs=[spec, _const_spec((1, d)), _const_spec((d, d)), _const_spec((d, d))],
        out_specs=[spec] * 4,
        out_shape=[jax.ShapeDtypeStruct((b, m, d), F32)] * 2 + [jax.ShapeDtypeStruct((b, m, d), BF16)] * 2,
        compiler_params=_params(("arbitrary",)),
        name="memory_kv",
    )(mem, ln_g.reshape(1, d), w_mk_b, w_mv_b)


def _merge_vals(x, o, bc, sa, sc, wpa, wpc, wout):
    mixed = sa.astype(F32) * _dot(o.astype(BF16), wpa) + sc.astype(F32) * _dot(bc, wpc)
    return x + _dot(mixed.astype(BF16), wout)


def _memory_attn_vals(qm, mk, mv):
    d = qm.shape[1]
    hd = d // MEM_HEADS
    outs = []
    for h in range(MEM_HEADS):
        sl = slice(h * hd, (h + 1) * hd)
        s = _dot_nt(qm[:, sl].astype(BF16), mk[:, sl]) * (hd ** -0.5)
        e = jnp.exp(s - jnp.max(s, axis=1, keepdims=True))
        p = e / jnp.sum(e, axis=1, keepdims=True)
        outs.append(_dot(p.astype(BF16), mv[:, sl]))
    return jnp.concatenate(outs, axis=1)


def _router_vals(h, wr_hi, wr_lo, br):
    h_hi = h.astype(BF16)
    h_lo = (h - h_hi.astype(F32)).astype(BF16)
    lg = _dot(h_hi, wr_hi) + (_dot(h_lo, wr_hi) + _dot(h_hi, wr_lo)) + br
    lane = lax.broadcasted_iota(jnp.int32, lg.shape, 1)
    big = jnp.int32(4 * LANES)
    is_g = (lane >= N_EXPERTS) & (lane < N_EXPERTS + N_GROUPS)
    gl = jnp.where(is_g, lg, NEG)
    gmax = jnp.max(gl, axis=1, keepdims=True)
    gidx = jnp.min(jnp.where(is_g & (gl == gmax), lane - N_EXPERTS, big), axis=1, keepdims=True)
    gprob = 1.0 / jnp.sum(jnp.where(is_g, jnp.exp(gl - gmax), 0.0), axis=1, keepdims=True)
    in_group = (lane < N_EXPERTS) & ((lane // EXPERTS_PER_GROUP) == gidx)
    el = jnp.where(in_group, lg, NEG)
    l1 = jnp.max(el, axis=1, keepdims=True)
    i1 = jnp.min(jnp.where(in_group & (el == l1), lane, big), axis=1, keepdims=True)
    rest = in_group & (lane != i1)
    el2 = jnp.where(rest, lg, NEG)
    l2 = jnp.max(el2, axis=1, keepdims=True)
    i2 = jnp.min(jnp.where(rest & (el2 == l2), lane, big), axis=1, keepdims=True)
    e21 = jnp.exp(l2 - l1)
    p1 = 1.0 / (1.0 + e21)
    g1 = gprob * p1
    g2 = gprob * (e21 * p1)
    return jnp.where(lane == 0, i1.astype(F32),
                     jnp.where(lane == 1, i2.astype(F32),
                               jnp.where(lane == 2, g1, jnp.where(lane == 3, g2, 0.0))))


def _post_vals(x1, om, wmo, gffn, wr_hi, wr_lo, br):
    x2 = x1 + _dot(om.astype(BF16), wmo)
    h = _rms(x2, gffn, EPS)
    return x2, h, _router_vals(h, wr_hi, wr_lo, br)


def _prompt_merge_kernel(x_ref, o_ref, bc_ref, sa_ref, sc_ref, mk_ref, mv_ref,
                         wpa_ref, wpc_ref, wout_ref, gmem_ref, wmq_ref, wmo_ref,
                         gffn_ref, wrh_ref, wrl_ref, br_ref,
                         x2_ref, h_ref, ri_ref):
    x1 = _merge_vals(x_ref[...], o_ref[...], bc_ref[...], sa_ref[...], sc_ref[...],
                     wpa_ref[...], wpc_ref[...], wout_ref[...])
    qm = _dot(_rms(x1, gmem_ref[...], EPS).astype(BF16), wmq_ref[...])
    om = _memory_attn_vals(qm, mk_ref[0], mv_ref[0])
    x2, h, ri = _post_vals(x1, om, wmo_ref[...], gffn_ref[...], wrh_ref[...], wrl_ref[...], br_ref[...])
    x2_ref[...] = x2
    h_ref[...] = h
    ri_ref[...] = ri


def _prompt_merge(x2d, o, bc, sa, sc, mk_b, mv_b, wts, *, seq_len):
    t, d = x2d.shape
    tm = min(MIX_ROWS, seq_len)
    assert seq_len % tm == 0
    seq_tiles = seq_len // tm
    n_mem = mk_b.shape[1]
    row = pl.BlockSpec((tm, d), lambda i: (i, 0))
    mem = pl.BlockSpec((1, n_mem, d), lambda i: (i // seq_tiles, 0, 0))
    return pl.pallas_call(
        _prompt_merge_kernel,
        grid=(t // tm,),
        in_specs=[row] * 5 + [mem] * 2 + [_const_spec(w.shape) for w in wts],
        out_specs=[row, row, pl.BlockSpec((tm, LANES), lambda i: (i, 0))],
        out_shape=[jax.ShapeDtypeStruct((t, d), F32), jax.ShapeDtypeStruct((t, d), F32),
                   jax.ShapeDtypeStruct((t, LANES), F32)],
        compiler_params=_params(("arbitrary",)),
        name="prompt_merge",
    )(x2d, o, bc, sa, sc, mk_b, mv_b, *wts)


def _sample_merge_kernel(x_ref, o_ref, bc_ref, sa_ref, sc_ref, wpa_ref, wpc_ref, wout_ref, gmem_ref, wmq_ref,
                         x1_ref, qm_ref):
    x1 = _merge_vals(x_ref[...], o_ref[...], bc_ref[...], sa_ref[...], sc_ref[...],
                     wpa_ref[...], wpc_ref[...], wout_ref[...])
    x1_ref[...] = x1
    qm_ref[...] = _dot(_rms(x1, gmem_ref[...], EPS).astype(BF16), wmq_ref[...])


def _sample_memattn_kernel(qm_ref, mk_ref, mv_ref, om_ref):
    om_ref[0] = _memory_attn_vals(qm_ref[0], mk_ref[0].astype(BF16), mv_ref[0].astype(BF16))


def _sample_post_kernel(x1_ref, om_ref, wmo_ref, gffn_ref, wrh_ref, wrl_ref, br_ref, x2_ref, h_ref, ri_ref):
    x2, h, ri = _post_vals(x1_ref[...], om_ref[...], wmo_ref[...], gffn_ref[...],
                           wrh_ref[...], wrl_ref[...], br_ref[...])
    x2_ref[...] = x2
    h_ref[...] = h
    ri_ref[...] = ri


def _sample_merge(x2d, o, bc, sa, sc, mem_k, mem_v, wts, *, n_seq):
    t, d = x2d.shape
    wpa, wpc, wout, gmem, wmq, wmo, gffn, wrh, wrl, br = wts
    full = pl.BlockSpec((t, d), lambda i: (0, 0))
    x1, qm = pl.pallas_call(
        _sample_merge_kernel,
        grid=(1,),
        in_specs=[full] * 5 + [_const_spec(w.shape) for w in (wpa, wpc, wout, gmem, wmq)],
        out_specs=[full, full],
        out_shape=[jax.ShapeDtypeStruct((t, d), F32)] * 2,
        compiler_params=_params(("arbitrary",)),
        name="sample_merge",
    )(x2d, o, bc, sa, sc, wpa, wpc, wout, gmem, wmq)
    rows = t // n_seq
    n_mem = mem_k.shape[1]
    seq = pl.BlockSpec((1, rows, d), lambda i: (i, 0, 0))
    mem = pl.BlockSpec((1, n_mem, d), lambda i: (i, 0, 0))
    om = pl.pallas_call(
        _sample_memattn_kernel,
        grid=(n_seq,),
        in_specs=[seq, mem, mem],
        out_specs=seq,
        out_shape=jax.ShapeDtypeStruct((n_seq, rows, d), F32),
        compiler_params=_params(("arbitrary",)),
        name="sample_memattn",
    )(qm.reshape(n_seq, rows, d), mem_k, mem_v)
    return pl.pallas_call(
        _sample_post_kernel,
        grid=(1,),
        in_specs=[full, full] + [_const_spec(w.shape) for w in (wmo, gffn, wrh, wrl, br)],
        out_specs=[full, full, pl.BlockSpec((t, LANES), lambda i: (0, 0))],
        out_shape=[jax.ShapeDtypeStruct((t, d), F32), jax.ShapeDtypeStruct((t, d), F32),
                   jax.ShapeDtypeStruct((t, LANES), F32)],
        compiler_params=_params(("arbitrary",)),
        name="sample_post",
    )(x1, om.reshape(t, d), wmo, gffn, wrh, wrl, br)


def _row_copy(src_hbm, idx, dst, r, sem):
    return pltpu.make_async_copy(src_hbm.at[pl.ds(idx, 1), :], dst.at[pl.ds(r, 1), :], sem)


def _moe_kernel(blk_e_ref, nused_ref, tok_ref, tok_next_ref, h_hbm, w1_ref, w3_ref, w2_ref, ys_ref, xbuf, sem):
    i = pl.program_id(0)
    nused = nused_ref[0]

    def gather(toks, slot):
        for r in range(MOE_BLOCK):
            _row_copy(h_hbm, toks[0, 0, r], xbuf.at[slot], r, sem.at[slot]).start()

    @pl.when(i == 0)
    def _():
        gather(tok_ref, 0)

    @pl.when(i + 1 < nused)
    def _():
        gather(tok_next_ref, (i + 1) % 2)

    @pl.when(i < nused)
    def _():
        slot = i % 2
        for r in range(MOE_BLOCK):
            _row_copy(h_hbm, 0, xbuf.at[slot], r, sem.at[slot]).wait()
        x = xbuf[slot].astype(BF16)
        a = _dot(x, w1_ref[0].astype(BF16))
        b = _dot(x, w3_ref[0].astype(BF16))
        ys_ref[...] = _dot((jax.nn.silu(a) * b).astype(BF16), w2_ref[0].astype(BF16))

    @pl.when(i >= nused)
    def _():
        ys_ref[...] = jnp.zeros(ys_ref.shape, F32)


def _moe_experts(h, slot_tok, blk_e, nused, w1, w3, w2):
    t, d = h.shape
    n_blocks = blk_e.shape[0]
    de = w1.shape[2]
    toks = slot_tok.reshape(n_blocks, 1, MOE_BLOCK)
    tok_spec = lambda f: pl.BlockSpec((1, 1, MOE_BLOCK), f, memory_space=pltpu.SMEM)
    grid_spec = pltpu.PrefetchScalarGridSpec(
        num_scalar_prefetch=2,
        grid=(n_blocks,),
        in_specs=[
            tok_spec(lambda i, be, nu: (i, 0, 0)),
            tok_spec(lambda i, be, nu: (jnp.minimum(i + 1, n_blocks - 1), 0, 0)),
            pl.BlockSpec(memory_space=pl.ANY),
            pl.BlockSpec((1, d, de), lambda i, be, nu: (be[i], 0, 0)),
            pl.BlockSpec((1, d, de), lambda i, be, nu: (be[i], 0, 0)),
            pl.BlockSpec((1, de, d), lambda i, be, nu: (be[i], 0, 0)),
        ],
        out_specs=pl.BlockSpec((MOE_BLOCK, d), lambda i, be, nu: (i, 0)),
        scratch_shapes=[pltpu.VMEM((2, MOE_BLOCK, d), F32), pltpu.SemaphoreType.DMA((2,))],
    )
    return pl.pallas_call(
        _moe_kernel,
        grid_spec=grid_spec,
        out_shape=jax.ShapeDtypeStruct((n_blocks * MOE_BLOCK, d), F32),
        compiler_params=_params(("arbitrary",)),
        name="moe_experts",
    )(blk_e, nused, toks, toks, h, w1, w3, w2)


def _combine_kernel(slot_ref, slot_next_ref, x2_ref, ri_ref, g_ref, ys_hbm, y_ref, ybuf, sem, *, tm):
    i = pl.program_id(0)
    n = pl.num_programs(0)

    def gather(slots, buf):
        for r in range(2 * tm):
            _row_copy(ys_hbm, slots[0, 0, r], ybuf.at[buf], r, sem.at[buf]).start()

    @pl.when(i == 0)
    def _():
        gather(slot_ref, 0)

    @pl.when(i + 1 < n)
    def _():
        gather(slot_next_ref, (i + 1) % 2)

    buf = i % 2
    for r in range(2 * tm):
        _row_copy(ys_hbm, 0, ybuf.at[buf], r, sem.at[buf]).wait()
    ri = ri_ref[...]
    y = x2_ref[...] + ri[:, 2:3] * ybuf[buf, 0:tm, :] + ri[:, 3:4] * ybuf[buf, tm:2 * tm, :]
    y_ref[...] = _rms(y, g_ref[...], EPS)


def _moe_combine(x2, rinfo, slots, ys, ln_g):
    t, d = x2.shape
    tm = min(COMBINE_ROWS, t)
    assert t % tm == 0
    n = t // tm
    sl = slots.reshape(n, tm, 2).transpose(0, 2, 1).reshape(n, 1, 2 * tm)
    slot_spec = lambda f: pl.BlockSpec((1, 1, 2 * tm), f, memory_space=pltpu.SMEM)
    row = pl.BlockSpec((tm, d), lambda i: (i, 0))
    return pl.pallas_call(
        functools.partial(_combine_kernel, tm=tm),
        grid=(n,),
        in_specs=[
            slot_spec(lambda i: (i, 0, 0)),
            slot_spec(lambda i: (jnp.minimum(i + 1, n - 1), 0, 0)),
            row,
            pl.BlockSpec((tm, LANES), lambda i: (i, 0)),
            _const_spec((1, d)),
            pl.BlockSpec(memory_space=pl.ANY),
        ],
        out_specs=row,
        out_shape=jax.ShapeDtypeStruct((t, d), F32),
        scratch_shapes=[pltpu.VMEM((2, 2 * tm, d), F32), pltpu.SemaphoreType.DMA((2,))],
        compiler_params=_params(("arbitrary",)),
        name="moe_combine",
    )(sl, sl, x2, rinfo, ln_g.reshape(1, d), ys)


def _moe_plan(rinfo):
    t = rinfo.shape[0]
    eid = rinfo[:, :EXPERT_TOP_K].astype(jnp.int32).reshape(-1)
    n_assign = t * EXPERT_TOP_K
    onehot = (eid[:, None] == jnp.arange(N_EXPERTS, dtype=jnp.int32)[None, :]).astype(jnp.int32)
    csum = jnp.cumsum(onehot, axis=0)
    rank = jnp.take_along_axis(csum, eid[:, None], axis=1)[:, 0] - 1
    counts = csum[-1]
    starts = jnp.cumsum(counts) - counts
    padded = (counts + MOE_BLOCK - 1) // MOE_BLOCK * MOE_BLOCK
    pends = jnp.cumsum(padded)
    pstarts = pends - padded
    slots = (pstarts[eid] + rank).reshape(t, EXPERT_TOP_K)
    n_blocks = -(-n_assign // MOE_BLOCK) + N_EXPERTS
    blk_start = jnp.arange(n_blocks, dtype=jnp.int32) * MOE_BLOCK
    blk_e = jnp.minimum(jnp.searchsorted(pends, blk_start, side='right'), N_EXPERTS - 1).astype(jnp.int32)
    order = jnp.argsort(eid, stable=True)
    s = jnp.arange(n_blocks * MOE_BLOCK, dtype=jnp.int32)
    e_s = jnp.repeat(blk_e, MOE_BLOCK)
    j = s - pstarts[e_s]
    valid = (j >= 0) & (j < counts[e_s])
    src = jnp.clip(starts[e_s] + j, 0, n_assign - 1)
    slot_tok = jnp.where(valid, order[src] // EXPERT_TOP_K, 0).astype(jnp.int32)
    nused = (pends[-1:] // MOE_BLOCK).astype(jnp.int32)
    return slots.astype(jnp.int32), slot_tok, blk_e, nused


def _moe_and_final(x2, h, rinfo, w1, w3, w2, ln_final_g):
    slots, slot_tok, blk_e, nused = _moe_plan(rinfo)
    ys = _moe_experts(h, slot_tok, blk_e, nused, w1, w3, w2)
    return _moe_combine(x2, rinfo, slots, ys, ln_final_g)


def _rope_tables(pos):
    half = HEAD_DIM // 2
    inv = 1.0 / (ROPE_THETA ** (jnp.arange(half, dtype=F32) / half))
    ang = pos.astype(F32)[:, None] * inv[None, :]
    cos = jnp.cos(ang)
    sin = jnp.sin(ang)
    cos_t = jnp.tile(jnp.concatenate([cos, cos], axis=1), (1, LANES // HEAD_DIM))
    sin_t = jnp.tile(jnp.concatenate([-sin, sin], axis=1), (1, LANES // HEAD_DIM))
    return cos_t, sin_t


def _lambda_init(layer_idx):
    return 0.8 - 0.6 * math.exp(-0.3 * layer_idx)


def _split_hi_lo(w):
    hi = w.astype(BF16)
    return hi, (w - hi.astype(F32)).astype(BF16)


def kernel(x_prompt, x_sample, cache_k, cache_v, state_conv, cache_mem_k, cache_mem_v, page_table, mem_prompt, ln_mix_g, w_in, lambda_q1, lambda_k1, lambda_q2, lambda_k2, subln_g, conv_w, w_proj_attn, w_proj_conv, w_out, ln_mem_g, ln_memkv_g, w_mq, w_mk, w_mv, w_mo, ln_ffn_g, w_group, b_group, w_router, b_router, w1, w3, w2, ln_final_g):
    b, s, d = x_prompt.shape
    db, dsq, _ = x_sample.shape
    depth = w_in.shape[0]
    n_pool, page = cache_k.shape[1], cache_k.shape[2]
    past = page_table.shape[1] * page
    assert d == 2 * N_HEADS * HEAD_DIM and w_in.shape[2] == 8 * d
    assert depth == 1, "only the last layer feeds the final norm; deeper stacks are not supported"
    assert CONV_WIDTH - 1 <= dsq <= SAMPLE_ROWS <= page
    pad = SAMPLE_ROWS - dsq

    cos_p, sin_p = _rope_tables(jnp.arange(s, dtype=jnp.int32))
    pos_s = past + jnp.minimum(jnp.arange(SAMPLE_ROWS, dtype=jnp.int32), dsq - 1)
    cos_s, sin_s = (jnp.tile(a, (db, 1)) for a in _rope_tables(pos_s))

    xp = x_prompt.reshape(b * s, d)
    xs = jnp.pad(x_sample, ((0, 0), (0, pad), (0, 0))).reshape(db * SAMPLE_ROWS, d)
    outs_p, outs_s = [], []
    for l in range(depth):
        lam_init = _lambda_init(l)
        lamv = jnp.pad(jnp.stack([lambda_q1[l], lambda_k1[l], lambda_q2[l], lambda_k2[l]]),
                       ((0, SUBLANES - 4), (0, LANES - HEAD_DIM)))
        w_in_b = w_in[l].astype(BF16)
        wr = jnp.pad(jnp.concatenate([w_router[l], w_group[l]], axis=1), ((0, 0), (0, LANES - N_EXPERTS - N_GROUPS)))
        br = jnp.pad(jnp.concatenate([b_router[l], b_group[l]]), (0, LANES - N_EXPERTS - N_GROUPS)).reshape(1, LANES)
        wr_hi, wr_lo = _split_hi_lo(wr)
        wts = (w_proj_attn[l].astype(BF16), w_proj_conv[l].astype(BF16), w_out[l].astype(BF16),
               ln_mem_g[l].reshape(1, d), w_mq[l].astype(BF16), w_mo[l].astype(BF16),
               ln_ffn_g[l].reshape(1, d), wr_hi, wr_lo, br)

        q, kf, kb, vf, vb, bc, sa, sc, tail = _mixer_in(xp, ln_mix_g[l], w_in_b, cos_p, sin_p, conv_w[l], seq_len=s)
        o = _prompt_attention(q.reshape(b, s, d), kb.reshape(b, s, d), vb.reshape(b, s, d), lamv, subln_g[l], lam_init)
        mkf, mvf, mkb, mvb = _memory_kv(mem_prompt, ln_memkv_g[l], w_mk[l].astype(BF16), w_mv[l].astype(BF16))
        x2, h, rinfo = _prompt_merge(xp, o.reshape(b * s, d), bc, sa, sc, mkb, mvb, wts, seq_len=s)
        nk_p = kf.reshape(b, s, 2 * N_HEADS, HEAD_DIM)
        nv_p = vf.reshape(b, s, N_HEADS, V_DIM)
        nc_p = tail[:, SUBLANES - (CONV_WIDTH - 1):, :]
        nmk_p = mkf.reshape(b, -1, MEM_HEADS, d // MEM_HEADS)
        nmv_p = mvf.reshape(b, -1, MEM_HEADS, d // MEM_HEADS)
        xp_moe = (x2, h, rinfo)

        st = state_conv[l]
        zrow = jnp.zeros((db, SAMPLE_ROWS - 1, d), F32)
        s1 = jnp.concatenate([st[:, 1:2], zrow], axis=1).reshape(db * SAMPLE_ROWS, d)
        s2 = jnp.concatenate([st, zrow[:, 1:]], axis=1).reshape(db * SAMPLE_ROWS, d)
        q, kf, kb, vf, vb, bc, sa, sc, u = _mixer_in(xs, ln_mix_g[l], w_in_b, cos_s, sin_s, conv_w[l],
                                                      seq_len=SAMPLE_ROWS, state_rows=(s1, s2))
        q4 = q.reshape(db, SAMPLE_ROWS, 2 * N_HEADS, HEAD_DIM)
        eye = jnp.eye(2 * N_HEADS, dtype=BF16)
        qbd = jnp.einsum('brhd,hg->bhdgr', q4, eye).reshape(db, d, 2 * N_HEADS * SAMPLE_ROWS)
        new_page = lambda a: jnp.pad(a.reshape(db, SAMPLE_ROWS, d), ((0, 0), (0, page - SAMPLE_ROWS), (0, 0)))
        o = _decode_attention(page_table, qbd, new_page(kf), new_page(vf),
                              cache_k[l].reshape(n_pool, page, d), cache_v[l].reshape(n_pool, page, d),
                              lamv, subln_g[l], lam_init, dsq)
        x2, h, rinfo = _sample_merge(xs, o.reshape(db * SAMPLE_ROWS, d), bc, sa, sc,
                                     cache_mem_k[l].reshape(db, -1, d), cache_mem_v[l].reshape(db, -1, d),
                                     wts, n_seq=db)
        real = lambda a: a.reshape(db, SAMPLE_ROWS, -1)[:, :dsq].reshape(db * dsq, -1)
        nk_s = real(kf).reshape(db, dsq, 2 * N_HEADS, HEAD_DIM)
        nv_s = real(vf).reshape(db, dsq, N_HEADS, V_DIM)
        nc_s = u.reshape(db, SAMPLE_ROWS, d)[:, dsq - (CONV_WIDTH - 1):dsq]
        xs_moe = (real(x2), real(h), real(rinfo))

        outs_p.append((nk_p, nv_p, nc_p, nmk_p, nmv_p))
        outs_s.append((nk_s, nv_s, nc_s))

    y_prompt = _moe_and_final(*xp_moe, w1[depth - 1], w3[depth - 1], w2[depth - 1], ln_final_g).reshape(b, s, d)
    y_sample = _moe_and_final(*xs_moe, w1[depth - 1], w3[depth - 1], w2[depth - 1], ln_final_g).reshape(db, dsq, d)
    stack = lambda items, j: jnp.stack([it[j] for it in items])
    return (y_prompt, y_sample,
            stack(outs_p, 0), stack(outs_p, 1), stack(outs_p, 2), stack(outs_p, 3), stack(outs_p, 4),
            stack(outs_s, 0), stack(outs_s, 1), stack(outs_s, 2))
```

```python
import functools
import math

import jax
import jax.numpy as jnp
from jax import lax
from jax.experimental import pallas as pl
from jax.experimental.pallas import tpu as pltpu

N_HEADS = 8
HEAD_DIM = 64
V_DIM = 2 * HEAD_DIM
ROPE_THETA = 10000.0
CONV_WIDTH = 3
MEM_HEADS = 4
N_GROUPS = 4
EXPERTS_PER_GROUP = 8
N_EXPERTS = N_GROUPS * EXPERTS_PER_GROUP
EXPERT_TOP_K = 2
MOE_BLOCK = 128
EPS = 1e-6
SUBLN_EPS = 1e-5

LANES = 128
SUBLANES = 8
VMEM_LIMIT_BYTES = 56 * 1024 * 1024

MIX_ROWS = 256
ATTN_TILE = 512
PAGES_PER_STEP = 8
SAMPLE_ROWS = 8
COMBINE_ROWS = 128

NEG = -1e30
Q_SCALE = HEAD_DIM ** -0.5 * math.log2(math.e)
F32 = jnp.float32
BF16 = jnp.bfloat16


def _params(sem):
    return pltpu.CompilerParams(dimension_semantics=sem, vmem_limit_bytes=VMEM_LIMIT_BYTES)


def _rms(x, g, eps):
    return x * lax.rsqrt(jnp.mean(x * x, axis=-1, keepdims=True) + eps) * g


def _dot(a, b):
    return jnp.dot(a, b, preferred_element_type=F32)


def _dot_nt(a, b):
    return lax.dot_general(a, b, (((1,), (1,)), ((), ())), preferred_element_type=F32)


def _const_spec(shape):
    nd = len(shape)
    return pl.BlockSpec(shape, lambda *_: (0,) * nd, pipeline_mode=pl.Buffered(1))


def _lambda_value(lamv, lam_init):
    a = jnp.sum(lamv[0:1, :] * lamv[1:2, :], axis=1, keepdims=True)
    b = jnp.sum(lamv[2:3, :] * lamv[3:4, :], axis=1, keepdims=True)
    return jnp.exp(a) - jnp.exp(b) + lam_init


def _head_norm(o, g, lam_init):
    return _rms(o, g, SUBLN_EPS) * (1.0 - lam_init)


def _mixer_in_kernel(*refs, tm, d, seq_tiles, period):
    if period:
        (x_ref, g_ref, w_ref, cos_ref, sin_ref, cw_ref, s1_ref, s2_ref,
         q_ref, kf_ref, kb_ref, vf_ref, vb_ref, bc_ref, sa_ref, sc_ref, tail_ref, ubuf_ref) = refs
    else:
        (x_ref, g_ref, w_ref, cos_ref, sin_ref, cw_ref,
         q_ref, kf_ref, kb_ref, vf_ref, vb_ref, bc_ref, sa_ref, sc_ref, tail_ref, ubuf_ref) = refs
    i = pl.program_id(0)
    h = _rms(x_ref[...], g_ref[...], EPS).astype(BF16)

    def proj(j):
        return _dot(h, w_ref[:, j * d:(j + 1) * d])

    cos = cos_ref[...]
    sin = sin_ref[...]
    lane = lax.broadcasted_iota(jnp.int32, (tm, LANES), 1)
    first_half = (lane % HEAD_DIM) < (HEAD_DIM // 2)

    def rope_chunks(z):
        for c in range(d // LANES):
            zc = z[:, c * LANES:(c + 1) * LANES]
            zr = jnp.where(first_half, pltpu.roll(zc, LANES - HEAD_DIM // 2, 1), pltpu.roll(zc, HEAD_DIM // 2, 1))
            yield c, zc * cos + zr * sin

    zq = proj(0)
    for c, y in rope_chunks(zq):
        q_ref[:, c * LANES:(c + 1) * LANES] = (y * Q_SCALE).astype(BF16)
    zk = proj(1)
    for c, y in rope_chunks(zk):
        kf_ref[:, c * LANES:(c + 1) * LANES] = y
        kb_ref[:, c * LANES:(c + 1) * LANES] = y.astype(BF16)
    zv = proj(2)
    vf_ref[...] = zv
    vb_ref[...] = zv.astype(BF16)

    b_gate = proj(3)
    u = proj(4) * proj(5)
    @pl.when(i % max(seq_tiles, 1) == 0)
    def _():
        ubuf_ref[0:SUBLANES, :] = jnp.zeros((SUBLANES, d), F32)
    ubuf_ref[SUBLANES:SUBLANES + tm, :] = u
    p1 = ubuf_ref[SUBLANES - 1:SUBLANES - 1 + tm, :]
    p2 = ubuf_ref[SUBLANES - 2:SUBLANES - 2 + tm, :]
    if period:
        r = lax.broadcasted_iota(jnp.int32, (tm, 1), 0) % period
        p1 = jnp.where(r == 0, s1_ref[...], p1)
        p2 = jnp.where(r < 2, s2_ref[...], p2)
    cw = cw_ref[...]
    y_conv = p2 * cw[0:1, :] + p1 * cw[1:2, :] + u * cw[2:3, :]
    bc_ref[...] = (b_gate * y_conv).astype(BF16)
    if seq_tiles:
        last = ubuf_ref[tm:tm + SUBLANES, :]
        ubuf_ref[0:SUBLANES, :] = last
        tail_ref[0] = last
    else:
        tail_ref[...] = u

    sa_ref[...] = jax.nn.sigmoid(proj(6)).astype(BF16)
    sc_ref[...] = jax.nn.sigmoid(proj(7)).astype(BF16)


def _mixer_in(x2d, ln_g, w_in_b, cos_t, sin_t, conv_w, *, seq_len, state_rows=None):
    t, d = x2d.shape
    tm = min(MIX_ROWS, t)
    assert t % tm == 0
    n_tiles = t // tm
    if state_rows is None:
        assert seq_len % tm == 0
        seq_tiles, period = seq_len // tm, 0
        n_seq = t // seq_len
        pos_tiles = seq_tiles
    else:
        assert tm % seq_len == 0 and n_tiles == 1
        seq_tiles, period = 0, seq_len
        pos_tiles = 1
    row = lambda i: (i, 0)
    pos = lambda i: (i % pos_tiles, 0)
    in_specs = [
        pl.BlockSpec((tm, d), row),
        _const_spec((1, d)),
        _const_spec(w_in_b.shape),
        pl.BlockSpec((tm, LANES), pos),
        pl.BlockSpec((tm, LANES), pos),
        _const_spec((SUBLANES, d)),
    ]
    args = [x2d, ln_g.reshape(1, d), w_in_b, cos_t, sin_t, jnp.pad(conv_w, ((0, SUBLANES - CONV_WIDTH), (0, 0)))]
    if period:
        in_specs += [pl.BlockSpec((tm, d), row), pl.BlockSpec((tm, d), row)]
        args += list(state_rows)
        tail_shape = jax.ShapeDtypeStruct((t, d), F32)
        tail_spec = pl.BlockSpec((tm, d), row)
    else:
        tail_shape = jax.ShapeDtypeStruct((n_seq, SUBLANES, d), F32)
        tail_spec = pl.BlockSpec((1, SUBLANES, d), lambda i: (i // seq_tiles, 0, 0))
    bspec = pl.BlockSpec((tm, d), row)
    sd = lambda dt: jax.ShapeDtypeStruct((t, d), dt)
    return pl.pallas_call(
        functools.partial(_mixer_in_kernel, tm=tm, d=d, seq_tiles=seq_tiles, period=period),
        grid=(n_tiles,),
        in_specs=in_specs,
        out_specs=[bspec] * 8 + [tail_spec],
        out_shape=[sd(BF16), sd(F32), sd(BF16), sd(F32), sd(BF16), sd(BF16), sd(BF16), sd(BF16), tail_shape],
        scratch_shapes=[pltpu.VMEM((tm + 2 * SUBLANES, d), F32)],
        compiler_params=_params(("arbitrary",)),
        name="mixer_in",
    )(*args)


def _prompt_attn_kernel(lamv_ref, g_ref, q_ref, k_ref, v_ref, o_ref,
                        s0_ref, s1_ref, mx0_ref, mx1_ref, m_ref, l_ref, acc_ref, *, tile, lam_init):
    qi = pl.program_id(2)
    q = q_ref[0]
    lane = lax.broadcasted_iota(jnp.int32, q.shape, 1)
    zero = jnp.zeros_like(q)
    qs = (jnp.where(lane < HEAD_DIM, q, zero), jnp.where(lane >= HEAD_DIM, q, zero))
    m_ref[...] = jnp.full(m_ref.shape, NEG, F32)
    l_ref[...] = jnp.zeros(l_ref.shape, F32)
    acc_ref[...] = jnp.zeros(acc_ref.shape, F32)
    n_chunks = tile // LANES

    s_bufs = (s0_ref, s1_ref)
    mx_bufs = (mx0_ref, mx1_ref)

    def scores(ki, buf, diagonal=False):
        off = pl.multiple_of(ki * tile, tile)
        k = k_ref[0, pl.ds(off, tile), :]
        for j in range(2):
            s = _dot_nt(qs[j], k)
            if diagonal:
                row = lax.broadcasted_iota(jnp.int32, (tile, tile), 0)
                col = lax.broadcasted_iota(jnp.int32, (tile, tile), 1)
                s = jnp.where(col <= row, s, NEG)
            s_bufs[buf][j] = s
            mx_bufs[buf][j] = functools.reduce(
                jnp.maximum, [s[:, c * LANES:(c + 1) * LANES] for c in range(n_chunks)])

    def absorb(ki, buf):
        off = pl.multiple_of(ki * tile, tile)
        v = v_ref[0, pl.ds(off, tile), :]
        for j in range(2):
            m_old = m_ref[j]
            m_new = jnp.maximum(m_old, jnp.max(mx_bufs[buf][j], axis=1, keepdims=True))
            alpha = jnp.exp2(m_old - m_new)
            ps = [jnp.exp2(s_bufs[buf][j, :, c * LANES:(c + 1) * LANES] - m_new) for c in range(n_chunks)]
            l_ref[j] = alpha * l_ref[j] + functools.reduce(lambda a, b: a + b, ps)
            p = jnp.concatenate(ps, axis=1).astype(BF16)
            acc_ref[j] = alpha * acc_ref[j] + _dot(p, v)
            m_ref[j] = m_new

    odd = qi % 2 == 1

    @pl.when(qi == 0)
    def _():
        scores(0, 0, diagonal=True)
        absorb(0, 0)

    @pl.when(qi > 0)
    def _():
        scores(0, 0)

    def pair(t, carry):
        i = 2 * t + 1
        scores(i, 1)
        absorb(i - 1, 0)
        scores(i + 1, 0)
        absorb(i, 1)
        return carry

    lax.fori_loop(0, (qi - 1) // 2, pair, 0)

    @pl.when(odd)
    def _():
        scores(qi, 1, diagonal=True)
        absorb(qi - 1, 0)
        absorb(qi, 1)

    @pl.when((qi > 0) & jnp.logical_not(odd))
    def _():
        scores(qi - 1, 1)
        absorb(qi - 2, 0)
        scores(qi, 0, diagonal=True)
        absorb(qi - 1, 1)
        absorb(qi, 0)

    lam = _lambda_value(lamv_ref[...], lam_init)
    l0 = jnp.sum(l_ref[0], axis=1, keepdims=True)
    l1 = jnp.sum(l_ref[1], axis=1, keepdims=True)
    o = acc_ref[0] / l0 - lam * (acc_ref[1] / l1)
    o_ref[0] = _head_norm(o, g_ref[...], lam_init).astype(BF16)


def _prompt_attention(q, k, v, lamv, subln_g, lam_init):
    b, s, d = q.shape
    tile = min(ATTN_TILE, s)
    assert s % tile == 0
    nq = s // tile
    return pl.pallas_call(
        functools.partial(_prompt_attn_kernel, tile=tile, lam_init=lam_init),
        grid=(b, N_HEADS, nq),
        in_specs=[
            _const_spec((SUBLANES, LANES)),
            _const_spec((1, V_DIM)),
            pl.BlockSpec((1, tile, LANES), lambda bi, h, qi: (bi, qi, h)),
            pl.BlockSpec((1, s, LANES), lambda bi, h, qi: (bi, 0, h)),
            pl.BlockSpec((1, s, LANES), lambda bi, h, qi: (bi, 0, h)),
        ],
        out_specs=pl.BlockSpec((1, tile, LANES), lambda bi, h, qi: (bi, qi, h)),
        out_shape=jax.ShapeDtypeStruct((b, s, d), BF16),
        scratch_shapes=[
            pltpu.VMEM((2, tile, tile), F32),
            pltpu.VMEM((2, tile, tile), F32),
            pltpu.VMEM((2, tile, LANES), F32),
            pltpu.VMEM((2, tile, LANES), F32),
            pltpu.VMEM((2, tile, LANES), F32),
            pltpu.VMEM((2, tile, LANES), F32),
            pltpu.VMEM((2, tile, V_DIM), F32),
        ],
        compiler_params=_params(("arbitrary", "arbitrary", "arbitrary")),
        name="prompt_attention",
    )(lamv, subln_g.reshape(1, V_DIM), q, k, v)


def _decode_attn_kernel(pt_ref, lamv_ref, g_ref, q_ref, kn_ref, vn_ref, *refs, n_pages_step, n_new, lam_init):
    k_refs = refs[:n_pages_step]
    v_refs = refs[n_pages_step:2 * n_pages_step]
    o_ref, m_ref, l_ref, acc_ref = refs[2 * n_pages_step:]
    p_id = pl.program_id(1)
    qbd = q_ref[0]
    n_cols = qbd.shape[1]
    cols_per_vh = 2 * SAMPLE_ROWS

    def rows_to_cols(x_row):
        return jnp.transpose(jnp.broadcast_to(x_row, (LANES, n_cols)))

    def update(k, v, visible=None):
        s = _dot(k.astype(BF16), qbd)
        if visible is not None:
            s = jnp.where(visible, s, NEG)
        m_old = m_ref[...]
        m_new = jnp.maximum(m_old, jnp.max(s, axis=0, keepdims=True))
        alpha = jnp.exp2(m_old - m_new)
        p = jnp.exp2(s - m_new)
        l_ref[...] = alpha * l_ref[...] + jnp.sum(p, axis=0, keepdims=True)
        m_ref[...] = m_new
        pt = jnp.transpose(p).astype(BF16)
        vb = v.astype(BF16)
        alpha_c = rows_to_cols(alpha)
        for vh in range(N_HEADS):
            rows = slice(vh * cols_per_vh, (vh + 1) * cols_per_vh)
            pv = _dot(pt[rows, :], vb[:, vh * V_DIM:(vh + 1) * V_DIM])
            acc_ref[rows, :] = alpha_c[rows, :] * acc_ref[rows, :] + pv

    @pl.when(p_id == 0)
    def _():
        m_ref[...] = jnp.full(m_ref.shape, NEG, F32)
        l_ref[...] = jnp.zeros(l_ref.shape, F32)
        acc_ref[...] = jnp.zeros(acc_ref.shape, F32)
        shape = (kn_ref.shape[1], n_cols)
        j = lax.broadcasted_iota(jnp.int32, shape, 0)
        r = lax.broadcasted_iota(jnp.int32, shape, 1) % SAMPLE_ROWS
        update(kn_ref[0], vn_ref[0], (j <= r) & (j < n_new))

    flat = lambda ref: pltpu.einshape("phd->p(hd)", ref[...])
    update(jnp.concatenate([flat(r) for r in k_refs], axis=0), jnp.concatenate([flat(r) for r in v_refs], axis=0))

    @pl.when(p_id == pl.num_programs(1) - 1)
    def _():
        lam = _lambda_value(lamv_ref[...], lam_init)
        linv_c = rows_to_cols(1.0 / l_ref[...])
        g = g_ref[...]
        for vh in range(N_HEADS):
            r1 = slice((2 * vh) * SAMPLE_ROWS, (2 * vh + 1) * SAMPLE_ROWS)
            r2 = slice((2 * vh + 1) * SAMPLE_ROWS, (2 * vh + 2) * SAMPLE_ROWS)
            o1 = acc_ref[r1, :] * linv_c[r1, :]
            o2 = acc_ref[r2, :] * linv_c[r2, :]
            o_ref[0, :, vh * V_DIM:(vh + 1) * V_DIM] = _head_norm(o1 - lam * o2, g, lam_init)


def _decode_attention(page_table, q2, k_new, v_new, cache_k, cache_v, layer, lamv, subln_g, lam_init, n_new):
    db, d, n_cols = q2.shape
    n_pages = page_table.shape[1]
    nps = math.gcd(PAGES_PER_STEP, n_pages)
    steps = n_pages // nps

    def page_spec(cache, j):
        return pl.BlockSpec((None, None) + cache.shape[2:], lambda b, p, pt: (layer, pt[b, p * nps + j], 0, 0, 0))

    per_seq = lambda a: pl.BlockSpec((1,) + a.shape[1:], lambda b, p, pt: (b,) + (0,) * (a.ndim - 1))
    grid_spec = pltpu.PrefetchScalarGridSpec(
        num_scalar_prefetch=1,
        grid=(db, steps),
        in_specs=[
            pl.BlockSpec((SUBLANES, LANES), lambda b, p, pt: (0, 0)),
            pl.BlockSpec((1, V_DIM), lambda b, p, pt: (0, 0)),
            per_seq(q2), per_seq(k_new), per_seq(v_new),
        ] + [page_spec(cache_k, j) for j in range(nps)] + [page_spec(cache_v, j) for j in range(nps)],
        out_specs=pl.BlockSpec((1, SAMPLE_ROWS, d), lambda b, p, pt: (b, 0, 0)),
        scratch_shapes=[
            pltpu.VMEM((1, n_cols), F32),
            pltpu.VMEM((1, n_cols), F32),
            pltpu.VMEM((n_cols, V_DIM), F32),
        ],
    )
    return pl.pallas_call(
        functools.partial(_decode_attn_kernel, n_pages_step=nps, n_new=n_new, lam_init=lam_init),
        grid_spec=grid_spec,
        out_shape=jax.ShapeDtypeStruct((db, SAMPLE_ROWS, d), F32),
        compiler_params=_params(("arbitrary", "arbitrary")),
        name="decode_attention",
    )(page_table, lamv, subln_g.reshape(1, V_DIM), q2, k_new, v_new,
      *([cache_k] * nps), *([cache_v] * nps))


def _memory_kv_kernel(mem_ref, g_ref, wk_ref, wv_ref, kf_ref, vf_ref, kb_ref, vb_ref):
    h = _rms(mem_ref[0], g_ref[...], EPS).astype(BF16)
    k = _dot(h, wk_ref[...])
    v = _dot(h, wv_ref[...])
    kf_ref[0] = k
    vf_ref[0] = v
    kb_ref[0] = k.astype(BF16)
    vb_ref[0] = v.astype(BF16)


def _memory_kv(mem, ln_g, w_mk_b, w_mv_b):
    b, m, d = mem.shape
    spec = pl.BlockSpec((1, m, d), lambda i: (i, 0, 0))
    return pl.pallas_call(
        _memory_kv_kernel,
        grid=(b,),
        in_specs=[spec, _const_spec((1, d)), _const_spec((d, d)), _const_spec((d, d))],
        out_specs=[spec] * 4,
        out_shape=[jax.ShapeDtypeStruct((b, m, d), F32)] * 2 + [jax.ShapeDtypeStruct((b, m, d), BF16)] * 2,
        compiler_params=_params(("arbitrary",)),
        name="memory_kv",
    )(mem, ln_g.reshape(1, d), w_mk_b, w_mv_b)


def _merge_vals(x, o, bc, sa, sc, wpa, wpc, wout):
    mixed = sa.astype(F32) * _dot(o.astype(BF16), wpa) + sc.astype(F32) * _dot(bc, wpc)
    return x + _dot(mixed.astype(BF16), wout)


def _memory_attn_vals(qm, mk, mv):
    d = qm.shape[1]
    hd = d // MEM_HEADS
    outs = []
    for h in range(MEM_HEADS):
        sl = slice(h * hd, (h + 1) * hd)
        s = _dot_nt(qm[:, sl].astype(BF16), mk[:, sl]) * (hd ** -0.5)
        e = jnp.exp(s - jnp.max(s, axis=1, keepdims=True))
        p = e / jnp.sum(e, axis=1, keepdims=True)
        outs.append(_dot(p.astype(BF16), mv[:, sl]))
    return jnp.concatenate(outs, axis=1)


def _router_vals(h, wr_hi, wr_lo, br):
    h_hi = h.astype(BF16)
    h_lo = (h - h_hi.astype(F32)).astype(BF16)
    lg = _dot(h_hi, wr_hi) + (_dot(h_lo, wr_hi) + _dot(h_hi, wr_lo)) + br
    lane = lax.broadcasted_iota(jnp.int32, lg.shape, 1)
    big = jnp.int32(4 * LANES)
    is_g = (lane >= N_EXPERTS) & (lane < N_EXPERTS + N_GROUPS)
    gl = jnp.where(is_g, lg, NEG)
    gmax = jnp.max(gl, axis=1, keepdims=True)
    gidx = jnp.min(jnp.where(is_g & (gl == gmax), lane - N_EXPERTS, big), axis=1, keepdims=True)
    gprob = 1.0 / jnp.sum(jnp.where(is_g, jnp.exp(gl - gmax), 0.0), axis=1, keepdims=True)
    in_group = (lane < N_EXPERTS) & ((lane // EXPERTS_PER_GROUP) == gidx)
    el = jnp.where(in_group, lg, NEG)
    l1 = jnp.max(el, axis=1, keepdims=True)
    i1 = jnp.min(jnp.where(in_group & (el == l1), lane, big), axis=1, keepdims=True)
    rest = in_group & (lane != i1)
    el2 = jnp.where(rest, lg, NEG)
    l2 = jnp.max(el2, axis=1, keepdims=True)
    i2 = jnp.min(jnp.where(rest & (el2 == l2), lane, big), axis=1, keepdims=True)
    e21 = jnp.exp(l2 - l1)
    p1 = 1.0 / (1.0 + e21)
    g1 = gprob * p1
    g2 = gprob * (e21 * p1)
    return jnp.where(lane == 0, i1.astype(F32),
                     jnp.where(lane == 1, i2.astype(F32),
                               jnp.where(lane == 2, g1, jnp.where(lane == 3, g2, 0.0))))


def _post_vals(x1, om, wmo, gffn, wr_hi, wr_lo, br):
    x2 = x1 + _dot(om.astype(BF16), wmo)
    h = _rms(x2, gffn, EPS)
    return x2, h, _router_vals(h, wr_hi, wr_lo, br)


def _prompt_merge_kernel(x_ref, o_ref, bc_ref, sa_ref, sc_ref, mk_ref, mv_ref,
                         wpa_ref, wpc_ref, wout_ref, gmem_ref, wmq_ref, wmo_ref,
                         gffn_ref, wrh_ref, wrl_ref, br_ref,
                         x2_ref, h_ref, ri_ref):
    x1 = _merge_vals(x_ref[...], o_ref[...], bc_ref[...], sa_ref[...], sc_ref[...],
                     wpa_ref[...], wpc_ref[...], wout_ref[...])
    qm = _dot(_rms(x1, gmem_ref[...], EPS).astype(BF16), wmq_ref[...])
    om = _memory_attn_vals(qm, mk_ref[0], mv_ref[0])
    x2, h, ri = _post_vals(x1, om, wmo_ref[...], gffn_ref[...], wrh_ref[...], wrl_ref[...], br_ref[...])
    x2_ref[...] = x2
    h_ref[...] = h
    ri_ref[...] = ri


def _prompt_merge(x2d, o, bc, sa, sc, mk_b, mv_b, wts, *, seq_len):
    t, d = x2d.shape
    tm = min(MIX_ROWS, seq_len)
    assert seq_len % tm == 0
    seq_tiles = seq_len // tm
    n_mem = mk_b.shape[1]
    row = pl.BlockSpec((tm, d), lambda i: (i, 0))
    mem = pl.BlockSpec((1, n_mem, d), lambda i: (i // seq_tiles, 0, 0))
    return pl.pallas_call(
        _prompt_merge_kernel,
        grid=(t // tm,),
        in_specs=[row] * 5 + [mem] * 2 + [_const_spec(w.shape) for w in wts],
        out_specs=[row, row, pl.BlockSpec((tm, LANES), lambda i: (i, 0))],
        out_shape=[jax.ShapeDtypeStruct((t, d), F32), jax.ShapeDtypeStruct((t, d), F32),
                   jax.ShapeDtypeStruct((t, LANES), F32)],
        compiler_params=_params(("arbitrary",)),
        name="prompt_merge",
    )(x2d, o, bc, sa, sc, mk_b, mv_b, *wts)


def _sample_merge_kernel(x_ref, o_ref, bc_ref, sa_ref, sc_ref, wpa_ref, wpc_ref, wout_ref, gmem_ref, wmq_ref,
                         x1_ref, qm_ref):
    x1 = _merge_vals(x_ref[...], o_ref[...], bc_ref[...], sa_ref[...], sc_ref[...],
                     wpa_ref[...], wpc_ref[...], wout_ref[...])
    x1_ref[...] = x1
    qm_ref[...] = _dot(_rms(x1, gmem_ref[...], EPS).astype(BF16), wmq_ref[...])


def _sample_memattn_kernel(qm_ref, mk_ref, mv_ref, om_ref):
    om_ref[0] = _memory_attn_vals(qm_ref[0], mk_ref[0].astype(BF16), mv_ref[0].astype(BF16))


def _sample_post_kernel(x1_ref, om_ref, wmo_ref, gffn_ref, wrh_ref, wrl_ref, br_ref, x2_ref, h_ref, ri_ref):
    x2, h, ri = _post_vals(x1_ref[...], om_ref[...], wmo_ref[...], gffn_ref[...],
                           wrh_ref[...], wrl_ref[...], br_ref[...])
    x2_ref[...] = x2
    h_ref[...] = h
    ri_ref[...] = ri


def _sample_merge(x2d, o, bc, sa, sc, mem_k, mem_v, wts, *, n_seq):
    t, d = x2d.shape
    wpa, wpc, wout, gmem, wmq, wmo, gffn, wrh, wrl, br = wts
    full = pl.BlockSpec((t, d), lambda i: (0, 0))
    x1, qm = pl.pallas_call(
        _sample_merge_kernel,
        grid=(1,),
        in_specs=[full] * 5 + [_const_spec(w.shape) for w in (wpa, wpc, wout, gmem, wmq)],
        out_specs=[full, full],
        out_shape=[jax.ShapeDtypeStruct((t, d), F32)] * 2,
        compiler_params=_params(("arbitrary",)),
        name="sample_merge",
    )(x2d, o, bc, sa, sc, wpa, wpc, wout, gmem, wmq)
    rows = t // n_seq
    n_mem = mem_k.shape[1]
    seq = pl.BlockSpec((1, rows, d), lambda i: (i, 0, 0))
    mem = pl.BlockSpec((1, n_mem, d), lambda i: (i, 0, 0))
    om = pl.pallas_call(
        _sample_memattn_kernel,
        grid=(n_seq,),
        in_specs=[seq, mem, mem],
        out_specs=seq,
        out_shape=jax.ShapeDtypeStruct((n_seq, rows, d), F32),
        compiler_params=_params(("arbitrary",)),
        name="sample_memattn",
    )(qm.reshape(n_seq, rows, d), mem_k, mem_v)
    return pl.pallas_call(
        _sample_post_kernel,
        grid=(1,),
        in_specs=[full, full] + [_const_spec(w.shape) for w in (wmo, gffn, wrh, wrl, br)],
        out_specs=[full, full, pl.BlockSpec((t, LANES), lambda i: (0, 0))],
        out_shape=[jax.ShapeDtypeStruct((t, d), F32), jax.ShapeDtypeStruct((t, d), F32),
                   jax.ShapeDtypeStruct((t, LANES), F32)],
        compiler_params=_params(("arbitrary",)),
        name="sample_post",
    )(x1, om.reshape(t, d), wmo, gffn, wrh, wrl, br)


def _row_copy(src_hbm, idx, dst, r, sem):
    return pltpu.make_async_copy(src_hbm.at[pl.ds(idx, 1), :], dst.at[pl.ds(r, 1), :], sem)


def _moe_kernel(blk_e_ref, nused_ref, tok_ref, tok_next_ref, h_hbm, w1_ref, w3_ref, w2_ref, ys_ref, xbuf, sem):
    i = pl.program_id(0)
    nused = nused_ref[0]

    def gather(toks, slot):
        for r in range(MOE_BLOCK):
            _row_copy(h_hbm, toks[0, 0, r], xbuf.at[slot], r, sem.at[slot]).start()

    @pl.when(i == 0)
    def _():
        gather(tok_ref, 0)

    @pl.when(i + 1 < nused)
    def _():
        gather(tok_next_ref, (i + 1) % 2)

    @pl.when(i < nused)
    def _():
        slot = i % 2
        for r in range(MOE_BLOCK):
            _row_copy(h_hbm, 0, xbuf.at[slot], r, sem.at[slot]).wait()
        x = xbuf[slot].astype(BF16)
        a = _dot(x, w1_ref[0].astype(BF16))
        b = _dot(x, w3_ref[0].astype(BF16))
        ys_ref[...] = _dot((jax.nn.silu(a) * b).astype(BF16), w2_ref[0].astype(BF16))

    @pl.when(i >= nused)
    def _():
        ys_ref[...] = jnp.zeros(ys_ref.shape, F32)


def _moe_experts(h, slot_tok, blk_e, nused, w1, w3, w2):
    t, d = h.shape
    n_blocks = blk_e.shape[0]
    de = w1.shape[2]
    toks = slot_tok.reshape(n_blocks, 1, MOE_BLOCK)
    tok_spec = lambda f: pl.BlockSpec((1, 1, MOE_BLOCK), f, memory_space=pltpu.SMEM)
    grid_spec = pltpu.PrefetchScalarGridSpec(
        num_scalar_prefetch=2,
        grid=(n_blocks,),
        in_specs=[
            tok_spec(lambda i, be, nu: (i, 0, 0)),
            tok_spec(lambda i, be, nu: (jnp.minimum(i + 1, n_blocks - 1), 0, 0)),
            pl.BlockSpec(memory_space=pl.ANY),
            pl.BlockSpec((1, d, de), lambda i, be, nu: (be[i], 0, 0)),
            pl.BlockSpec((1, d, de), lambda i, be, nu: (be[i], 0, 0)),
            pl.BlockSpec((1, de, d), lambda i, be, nu: (be[i], 0, 0)),
        ],
        out_specs=pl.BlockSpec((MOE_BLOCK, d), lambda i, be, nu: (i, 0)),
        scratch_shapes=[pltpu.VMEM((2, MOE_BLOCK, d), F32), pltpu.SemaphoreType.DMA((2,))],
    )
    return pl.pallas_call(
        _moe_kernel,
        grid_spec=grid_spec,
        out_shape=jax.ShapeDtypeStruct((n_blocks * MOE_BLOCK, d), F32),
        compiler_params=_params(("arbitrary",)),
        name="moe_experts",
    )(blk_e, nused, toks, toks, h, w1, w3, w2)


def _combine_kernel(slot_ref, slot_next_ref, x2_ref, ri_ref, g_ref, ys_hbm, y_ref, ybuf, sem, *, tm):
    i = pl.program_id(0)
    n = pl.num_programs(0)

    def gather(slots, buf):
        for r in range(2 * tm):
            _row_copy(ys_hbm, slots[0, 0, r], ybuf.at[buf], r, sem.at[buf]).start()

    @pl.when(i == 0)
    def _():
        gather(slot_ref, 0)

    @pl.when(i + 1 < n)
    def _():
        gather(slot_next_ref, (i + 1) % 2)

    buf = i % 2
    for r in range(2 * tm):
        _row_copy(ys_hbm, 0, ybuf.at[buf], r, sem.at[buf]).wait()
    ri = ri_ref[...]
    y = x2_ref[...] + ri[:, 2:3] * ybuf[buf, 0:tm, :] + ri[:, 3:4] * ybuf[buf, tm:2 * tm, :]
    y_ref[...] = _rms(y, g_ref[...], EPS)


def _moe_combine(x2, rinfo, slots, ys, ln_g):
    t, d = x2.shape
    tm = min(COMBINE_ROWS, t)
    assert t % tm == 0
    n = t // tm
    sl = slots.reshape(n, tm, 2).transpose(0, 2, 1).reshape(n, 1, 2 * tm)
    slot_spec = lambda f: pl.BlockSpec((1, 1, 2 * tm), f, memory_space=pltpu.SMEM)
    row = pl.BlockSpec((tm, d), lambda i: (i, 0))
    return pl.pallas_call(
        functools.partial(_combine_kernel, tm=tm),
        grid=(n,),
        in_specs=[
            slot_spec(lambda i: (i, 0, 0)),
            slot_spec(lambda i: (jnp.minimum(i + 1, n - 1), 0, 0)),
            row,
            pl.BlockSpec((tm, LANES), lambda i: (i, 0)),
            _const_spec((1, d)),
            pl.BlockSpec(memory_space=pl.ANY),
        ],
        out_specs=row,
        out_shape=jax.ShapeDtypeStruct((t, d), F32),
        scratch_shapes=[pltpu.VMEM((2, 2 * tm, d), F32), pltpu.SemaphoreType.DMA((2,))],
        compiler_params=_params(("arbitrary",)),
        name="moe_combine",
    )(sl, sl, x2, rinfo, ln_g.reshape(1, d), ys)


def _moe_plan(rinfo):
    t = rinfo.shape[0]
    eid = rinfo[:, :EXPERT_TOP_K].astype(jnp.int32).reshape(-1)
    n_assign = t * EXPERT_TOP_K
    onehot = (eid[:, None] == jnp.arange(N_EXPERTS, dtype=jnp.int32)[None, :]).astype(jnp.int32)
    csum = jnp.cumsum(onehot, axis=0)
    rank = jnp.take_along_axis(csum, eid[:, None], axis=1)[:, 0] - 1
    counts = csum[-1]
    starts = jnp.cumsum(counts) - counts
    padded = (counts + MOE_BLOCK - 1) // MOE_BLOCK * MOE_BLOCK
    pends = jnp.cumsum(padded)
    pstarts = pends - padded
    slots = (pstarts[eid] + rank).reshape(t, EXPERT_TOP_K)
    n_blocks = -(-n_assign // MOE_BLOCK) + N_EXPERTS
    blk_start = jnp.arange(n_blocks, dtype=jnp.int32) * MOE_BLOCK
    blk_e = jnp.minimum(jnp.searchsorted(pends, blk_start, side='right'), N_EXPERTS - 1).astype(jnp.int32)
    order = jnp.argsort(eid, stable=True)
    s = jnp.arange(n_blocks * MOE_BLOCK, dtype=jnp.int32)
    e_s = jnp.repeat(blk_e, MOE_BLOCK)
    j = s - pstarts[e_s]
    valid = (j >= 0) & (j < counts[e_s])
    src = jnp.clip(starts[e_s] + j, 0, n_assign - 1)
    slot_tok = jnp.where(valid, order[src] // EXPERT_TOP_K, 0).astype(jnp.int32)
    nused = (pends[-1:] // MOE_BLOCK).astype(jnp.int32)
    return slots.astype(jnp.int32), slot_tok, blk_e, nused


def _moe_and_final(x2, h, rinfo, w1, w3, w2, ln_final_g):
    slots, slot_tok, blk_e, nused = _moe_plan(rinfo)
    ys = _moe_experts(h, slot_tok, blk_e, nused, w1, w3, w2)
    return _moe_combine(x2, rinfo, slots, ys, ln_final_g)


def _rope_tables(pos):
    half = HEAD_DIM // 2
    inv = 1.0 / (ROPE_THETA ** (jnp.arange(half, dtype=F32) / half))
    ang = pos.astype(F32)[:, None] * inv[None, :]
    cos = jnp.cos(ang)
    sin = jnp.sin(ang)
    cos_t = jnp.tile(jnp.concatenate([cos, cos], axis=1), (1, LANES // HEAD_DIM))
    sin_t = jnp.tile(jnp.concatenate([-sin, sin], axis=1), (1, LANES // HEAD_DIM))
    return cos_t, sin_t


def _lambda_init(layer_idx):
    return 0.8 - 0.6 * math.exp(-0.3 * layer_idx)


def _split_hi_lo(w):
    hi = w.astype(BF16)
    return hi, (w - hi.astype(F32)).astype(BF16)


def kernel(x_prompt, x_sample, cache_k, cache_v, state_conv, cache_mem_k, cache_mem_v, page_table, mem_prompt, ln_mix_g, w_in, lambda_q1, lambda_k1, lambda_q2, lambda_k2, subln_g, conv_w, w_proj_attn, w_proj_conv, w_out, ln_mem_g, ln_memkv_g, w_mq, w_mk, w_mv, w_mo, ln_ffn_g, w_group, b_group, w_router, b_router, w1, w3, w2, ln_final_g):
    b, s, d = x_prompt.shape
    db, dsq, _ = x_sample.shape
    depth = w_in.shape[0]
    n_pool, page = cache_k.shape[1], cache_k.shape[2]
    past = page_table.shape[1] * page
    assert d == 2 * N_HEADS * HEAD_DIM and w_in.shape[2] == 8 * d
    assert depth == 1, "only the last layer feeds the final norm; deeper stacks are not supported"
    assert CONV_WIDTH - 1 <= dsq <= SAMPLE_ROWS <= page
    pad = SAMPLE_ROWS - dsq

    cos_p, sin_p = _rope_tables(jnp.arange(s, dtype=jnp.int32))
    pos_s = past + jnp.minimum(jnp.arange(SAMPLE_ROWS, dtype=jnp.int32), dsq - 1)
    cos_s, sin_s = (jnp.tile(a, (db, 1)) for a in _rope_tables(pos_s))

    xp = x_prompt.reshape(b * s, d)
    xs = jnp.pad(x_sample, ((0, 0), (0, pad), (0, 0))).reshape(db * SAMPLE_ROWS, d)
    outs_p, outs_s = [], []
    for l in range(depth):
        lam_init = _lambda_init(l)
        lamv = jnp.pad(jnp.stack([lambda_q1[l], lambda_k1[l], lambda_q2[l], lambda_k2[l]]),
                       ((0, SUBLANES - 4), (0, LANES - HEAD_DIM)))
        w_in_b = w_in[l].astype(BF16)
        wr = jnp.pad(jnp.concatenate([w_router[l], w_group[l]], axis=1), ((0, 0), (0, LANES - N_EXPERTS - N_GROUPS)))
        br = jnp.pad(jnp.concatenate([b_router[l], b_group[l]]), (0, LANES - N_EXPERTS - N_GROUPS)).reshape(1, LANES)
        wr_hi, wr_lo = _split_hi_lo(wr)
        wts = (w_proj_attn[l].astype(BF16), w_proj_conv[l].astype(BF16), w_out[l].astype(BF16),
               ln_mem_g[l].reshape(1, d), w_mq[l].astype(BF16), w_mo[l].astype(BF16),
               ln_ffn_g[l].reshape(1, d), wr_hi, wr_lo, br)

        q, kf, kb, vf, vb, bc, sa, sc, tail = _mixer_in(xp, ln_mix_g[l], w_in_b, cos_p, sin_p, conv_w[l], seq_len=s)
        o = _prompt_attention(q.reshape(b, s, d), kb.reshape(b, s, d), vb.reshape(b, s, d), lamv, subln_g[l], lam_init)
        mkf, mvf, mkb, mvb = _memory_kv(mem_prompt, ln_memkv_g[l], w_mk[l].astype(BF16), w_mv[l].astype(BF16))
        x2, h, rinfo = _prompt_merge(xp, o.reshape(b * s, d), bc, sa, sc, mkb, mvb, wts, seq_len=s)
        nk_p = kf.reshape(b, s, 2 * N_HEADS, HEAD_DIM)
        nv_p = vf.reshape(b, s, N_HEADS, V_DIM)
        nc_p = tail[:, SUBLANES - (CONV_WIDTH - 1):, :]
        nmk_p = mkf.reshape(b, -1, MEM_HEADS, d // MEM_HEADS)
        nmv_p = mvf.reshape(b, -1, MEM_HEADS, d // MEM_HEADS)
        xp_moe = (x2, h, rinfo)

        st = state_conv[l]
        zrow = jnp.zeros((db, SAMPLE_ROWS - 1, d), F32)
        s1 = jnp.concatenate([st[:, 1:2], zrow], axis=1).reshape(db * SAMPLE_ROWS, d)
        s2 = jnp.concatenate([st, zrow[:, 1:]], axis=1).reshape(db * SAMPLE_ROWS, d)
        q, kf, kb, vf, vb, bc, sa, sc, u = _mixer_in(xs, ln_mix_g[l], w_in_b, cos_s, sin_s, conv_w[l],
                                                      seq_len=SAMPLE_ROWS, state_rows=(s1, s2))
        q4 = q.reshape(db, SAMPLE_ROWS, 2 * N_HEADS, HEAD_DIM)
        eye = jnp.eye(2 * N_HEADS, dtype=BF16)
        q2 = jnp.einsum('brhd,hg->bhdgr', q4, eye).reshape(db, d, 2 * N_HEADS * SAMPLE_ROWS)
        new_page = lambda a: jnp.pad(a.reshape(db, SAMPLE_ROWS, d), ((0, 0), (0, page - SAMPLE_ROWS), (0, 0)))
        o = _decode_attention(page_table, q2, new_page(kf), new_page(vf), cache_k, cache_v, l,
                              lamv, subln_g[l], lam_init, dsq)
        x2, h, rinfo = _sample_merge(xs, o.reshape(db * SAMPLE_ROWS, d), bc, sa, sc,
                                     cache_mem_k[l].reshape(db, -1, d), cache_mem_v[l].reshape(db, -1, d),
                                     wts, n_seq=db)
        real = lambda a: a.reshape(db, SAMPLE_ROWS, -1)[:, :dsq].reshape(db * dsq, -1)
        nk_s = real(kf).reshape(db, dsq, 2 * N_HEADS, HEAD_DIM)
        nv_s = real(vf).reshape(db, dsq, N_HEADS, V_DIM)
        nc_s = u.reshape(db, SAMPLE_ROWS, d)[:, dsq - (CONV_WIDTH - 1):dsq]
        xs_moe = (real(x2), real(h), real(rinfo))

        outs_p.append((nk_p, nv_p, nc_p, nmk_p, nmv_p))
        outs_s.append((nk_s, nv_s, nc_s))

    y_prompt = _moe_and_final(*xp_moe, w1[depth - 1], w3[depth - 1], w2[depth - 1], ln_final_g).reshape(b, s, d)
    y_sample = _moe_and_final(*xs_moe, w1[depth - 1], w3[depth - 1], w2[depth - 1], ln_final_g).reshape(db, dsq, d)
    stack = lambda items, j: jnp.stack([it[j] for it in items])
    return (y_prompt, y_sample,
            stack(outs_p, 0), stack(outs_p, 1), stack(outs_p, 2), stack(outs_p, 3), stack(outs_p, 4),
            stack(outs_s, 0), stack(outs_s, 1), stack(outs_s, 2))
```

```python
import functools
import math

import jax
import jax.numpy as jnp
from jax import lax
from jax.experimental import pallas as pl
from jax.experimental.pallas import tpu as pltpu

N_HEADS = 8
HEAD_DIM = 64
V_DIM = 2 * HEAD_DIM
ROPE_THETA = 10000.0
CONV_WIDTH = 3
MEM_HEADS = 4
N_GROUPS = 4
EXPERTS_PER_GROUP = 8
N_EXPERTS = N_GROUPS * EXPERTS_PER_GROUP
EXPERT_TOP_K = 2
MOE_BLOCK = 128
EPS = 1e-6
SUBLN_EPS = 1e-5

LANES = 128
SUBLANES = 8
VMEM_LIMIT_BYTES = 56 * 1024 * 1024

MIX_ROWS = 256
ATTN_TILE = 512
PAGES_PER_STEP = 8
SAMPLE_ROWS = 8
COMBINE_ROWS = 128

NEG = -1e30
Q_SCALE = HEAD_DIM ** -0.5 * math.log2(math.e)
F32 = jnp.float32
BF16 = jnp.bfloat16


def _params(sem):
    return pltpu.CompilerParams(dimension_semantics=sem, vmem_limit_bytes=VMEM_LIMIT_BYTES)


def _rms(x, g, eps):
    return x * lax.rsqrt(jnp.mean(x * x, axis=-1, keepdims=True) + eps) * g


def _dot(a, b):
    return jnp.dot(a, b, preferred_element_type=F32)


def _dot_nt(a, b):
    return lax.dot_general(a, b, (((1,), (1,)), ((), ())), preferred_element_type=F32)


def _const_spec(shape):
    nd = len(shape)
    return pl.BlockSpec(shape, lambda *_: (0,) * nd, pipeline_mode=pl.Buffered(1))


def _lambda_value(lamv, lam_init):
    a = jnp.sum(lamv[0:1, :] * lamv[1:2, :], axis=1, keepdims=True)
    b = jnp.sum(lamv[2:3, :] * lamv[3:4, :], axis=1, keepdims=True)
    return jnp.exp(a) - jnp.exp(b) + lam_init


def _head_norm(o, g, lam_init):
    return _rms(o, g, SUBLN_EPS) * (1.0 - lam_init)


def _mixer_in_kernel(*refs, tm, d, seq_tiles, period):
    if period:
        (x_ref, g_ref, w_ref, cos_ref, sin_ref, cw_ref, s1_ref, s2_ref,
         q_ref, kf_ref, kb_ref, vf_ref, vb_ref, bc_ref, sa_ref, sc_ref, tail_ref, ubuf_ref) = refs
    else:
        (x_ref, g_ref, w_ref, cos_ref, sin_ref, cw_ref,
         q_ref, kf_ref, kb_ref, vf_ref, vb_ref, bc_ref, sa_ref, sc_ref, tail_ref, ubuf_ref) = refs
    i = pl.program_id(0)
    h = _rms(x_ref[...], g_ref[...], EPS).astype(BF16)

    def proj(j):
        return _dot(h, w_ref[:, j * d:(j + 1) * d])

    cos = cos_ref[...]
    sin = sin_ref[...]
    lane = lax.broadcasted_iota(jnp.int32, (tm, LANES), 1)
    first_half = (lane % HEAD_DIM) < (HEAD_DIM // 2)

    def rope_chunks(z):
        for c in range(d // LANES):
            zc = z[:, c * LANES:(c + 1) * LANES]
            zr = jnp.where(first_half, pltpu.roll(zc, LANES - HEAD_DIM // 2, 1), pltpu.roll(zc, HEAD_DIM // 2, 1))
            yield c, zc * cos + zr * sin

    zq = proj(0)
    for c, y in rope_chunks(zq):
        q_ref[:, c * LANES:(c + 1) * LANES] = (y * Q_SCALE).astype(BF16)
    zk = proj(1)
    for c, y in rope_chunks(zk):
        if seq_tiles:
            kf_ref[0, c * LANES:(c + 1) * LANES, :] = y.T
        else:
            kf_ref[:, c * LANES:(c + 1) * LANES] = y
        kb_ref[:, c * LANES:(c + 1) * LANES] = y.astype(BF16)
    zv = proj(2)
    vf_ref[...] = zv
    vb_ref[...] = zv.astype(BF16)

    b_gate = proj(3)
    u = proj(4) * proj(5)
    @pl.when(i % max(seq_tiles, 1) == 0)
    def _():
        ubuf_ref[0:SUBLANES, :] = jnp.zeros((SUBLANES, d), F32)
    ubuf_ref[SUBLANES:SUBLANES + tm, :] = u
    p1 = ubuf_ref[SUBLANES - 1:SUBLANES - 1 + tm, :]
    p2 = ubuf_ref[SUBLANES - 2:SUBLANES - 2 + tm, :]
    if period:
        r = lax.broadcasted_iota(jnp.int32, (tm, 1), 0) % period
        p1 = jnp.where(r == 0, s1_ref[...], p1)
        p2 = jnp.where(r < 2, s2_ref[...], p2)
    cw = cw_ref[...]
    y_conv = p2 * cw[0:1, :] + p1 * cw[1:2, :] + u * cw[2:3, :]
    bc_ref[...] = (b_gate * y_conv).astype(BF16)
    if seq_tiles:
        last = ubuf_ref[tm:tm + SUBLANES, :]
        ubuf_ref[0:SUBLANES, :] = last
        tail_ref[0] = last
    else:
        tail_ref[...] = u

    sa_ref[...] = jax.nn.sigmoid(proj(6)).astype(BF16)
    sc_ref[...] = jax.nn.sigmoid(proj(7)).astype(BF16)


def _mixer_in(x2d, ln_g, w_in_b, cos_t, sin_t, conv_w, *, seq_len, state_rows=None):
    t, d = x2d.shape
    tm = min(MIX_ROWS, t)
    assert t % tm == 0
    n_tiles = t // tm
    if state_rows is None:
        assert seq_len % tm == 0
        seq_tiles, period = seq_len // tm, 0
        n_seq = t // seq_len
        pos_tiles = seq_tiles
    else:
        assert tm % seq_len == 0 and n_tiles == 1
        seq_tiles, period = 0, seq_len
        pos_tiles = 1
    row = lambda i: (i, 0)
    pos = lambda i: (i % pos_tiles, 0)
    in_specs = [
        pl.BlockSpec((tm, d), row),
        _const_spec((1, d)),
        _const_spec(w_in_b.shape),
        pl.BlockSpec((tm, LANES), pos),
        pl.BlockSpec((tm, LANES), pos),
        _const_spec((SUBLANES, d)),
    ]
    args = [x2d, ln_g.reshape(1, d), w_in_b, cos_t, sin_t, jnp.pad(conv_w, ((0, SUBLANES - CONV_WIDTH), (0, 0)))]
    bspec = pl.BlockSpec((tm, d), row)
    sd = lambda dt: jax.ShapeDtypeStruct((t, d), dt)
    if period:
        in_specs += [pl.BlockSpec((tm, d), row), pl.BlockSpec((tm, d), row)]
        args += list(state_rows)
        tail_shape = jax.ShapeDtypeStruct((t, d), F32)
        tail_spec = pl.BlockSpec((tm, d), row)
        kf_shape, kf_spec = sd(F32), bspec
    else:
        tail_shape = jax.ShapeDtypeStruct((n_seq, SUBLANES, d), F32)
        tail_spec = pl.BlockSpec((1, SUBLANES, d), lambda i: (i // seq_tiles, 0, 0))
        kf_shape = jax.ShapeDtypeStruct((n_seq, d, seq_len), F32)
        kf_spec = pl.BlockSpec((1, d, tm), lambda i: (i // seq_tiles, 0, i % seq_tiles))
    return pl.pallas_call(
        functools.partial(_mixer_in_kernel, tm=tm, d=d, seq_tiles=seq_tiles, period=period),
        grid=(n_tiles,),
        in_specs=in_specs,
        out_specs=[bspec, kf_spec] + [bspec] * 6 + [tail_spec],
        out_shape=[sd(BF16), kf_shape, sd(BF16), sd(F32), sd(BF16), sd(BF16), sd(BF16), sd(BF16), tail_shape],
        scratch_shapes=[pltpu.VMEM((tm + 2 * SUBLANES, d), F32)],
        compiler_params=_params(("arbitrary",)),
        name="mixer_in",
    )(*args)


def _prompt_attn_kernel(lamv_ref, g_ref, q_ref, k_ref, v_ref, o_ref,
                        s0_ref, s1_ref, mx0_ref, mx1_ref, m_ref, l_ref, acc_ref, *, tile, lam_init):
    qi = pl.program_id(2)
    q = q_ref[0]
    lane = lax.broadcasted_iota(jnp.int32, q.shape, 1)
    zero = jnp.zeros_like(q)
    qs = (jnp.where(lane < HEAD_DIM, q, zero), jnp.where(lane >= HEAD_DIM, q, zero))
    m_ref[...] = jnp.full(m_ref.shape, NEG, F32)
    l_ref[...] = jnp.zeros(l_ref.shape, F32)
    acc_ref[...] = jnp.zeros(acc_ref.shape, F32)
    n_chunks = tile // LANES

    s_bufs = (s0_ref, s1_ref)
    mx_bufs = (mx0_ref, mx1_ref)

    def scores(ki, buf, diagonal=False):
        off = pl.multiple_of(ki * tile, tile)
        k = k_ref[0, pl.ds(off, tile), :]
        for j in range(2):
            s = _dot_nt(qs[j], k)
            if diagonal:
                row = lax.broadcasted_iota(jnp.int32, (tile, tile), 0)
                col = lax.broadcasted_iota(jnp.int32, (tile, tile), 1)
                s = jnp.where(col <= row, s, NEG)
            s_bufs[buf][j] = s
            mx_bufs[buf][j] = functools.reduce(
                jnp.maximum, [s[:, c * LANES:(c + 1) * LANES] for c in range(n_chunks)])

    def absorb(ki, buf):
        off = pl.multiple_of(ki * tile, tile)
        v = v_ref[0, pl.ds(off, tile), :]
        for j in range(2):
            m_old = m_ref[j]
            m_new = jnp.maximum(m_old, jnp.max(mx_bufs[buf][j], axis=1, keepdims=True))
            alpha = jnp.exp2(m_old - m_new)
            ps = [jnp.exp2(s_bufs[buf][j, :, c * LANES:(c + 1) * LANES] - m_new) for c in range(n_chunks)]
            l_ref[j] = alpha * l_ref[j] + functools.reduce(lambda a, b: a + b, ps)
            p = jnp.concatenate(ps, axis=1).astype(BF16)
            acc_ref[j] = alpha * acc_ref[j] + _dot(p, v)
            m_ref[j] = m_new

    odd = qi % 2 == 1

    @pl.when(qi == 0)
    def _():
        scores(0, 0, diagonal=True)
        absorb(0, 0)

    @pl.when(qi > 0)
    def _():
        scores(0, 0)

    def pair(t, carry):
        i = 2 * t + 1
        scores(i, 1)
        absorb(i - 1, 0)
        scores(i + 1, 0)
        absorb(i, 1)
        return carry

    lax.fori_loop(0, (qi - 1) // 2, pair, 0)

    @pl.when(odd)
    def _():
        scores(qi, 1, diagonal=True)
        absorb(qi - 1, 0)
        absorb(qi, 1)

    @pl.when((qi > 0) & jnp.logical_not(odd))
    def _():
        scores(qi - 1, 1)
        absorb(qi - 2, 0)
        scores(qi, 0, diagonal=True)
        absorb(qi - 1, 1)
        absorb(qi, 0)

    lam = _lambda_value(lamv_ref[...], lam_init)
    l0 = jnp.sum(l_ref[0], axis=1, keepdims=True)
    l1 = jnp.sum(l_ref[1], axis=1, keepdims=True)
    o = acc_ref[0] / l0 - lam * (acc_ref[1] / l1)
    o_ref[0] = _head_norm(o, g_ref[...], lam_init).astype(BF16)


def _prompt_attention(q, k, v, lamv, subln_g, lam_init):
    b, s, d = q.shape
    tile = min(ATTN_TILE, s)
    assert s % tile == 0
    nq = s // tile
    return pl.pallas_call(
        functools.partial(_prompt_attn_kernel, tile=tile, lam_init=lam_init),
        grid=(b, N_HEADS, nq),
        in_specs=[
            _const_spec((SUBLANES, LANES)),
            _const_spec((1, V_DIM)),
            pl.BlockSpec((1, tile, LANES), lambda bi, h, qi: (bi, qi, h)),
            pl.BlockSpec((1, s, LANES), lambda bi, h, qi: (bi, 0, h)),
            pl.BlockSpec((1, s, LANES), lambda bi, h, qi: (bi, 0, h)),
        ],
        out_specs=pl.BlockSpec((1, tile, LANES), lambda bi, h, qi: (bi, qi, h)),
        out_shape=jax.ShapeDtypeStruct((b, s, d), BF16),
        scratch_shapes=[
            pltpu.VMEM((2, tile, tile), F32),
            pltpu.VMEM((2, tile, tile), F32),
            pltpu.VMEM((2, tile, LANES), F32),
            pltpu.VMEM((2, tile, LANES), F32),
            pltpu.VMEM((2, tile, LANES), F32),
            pltpu.VMEM((2, tile, LANES), F32),
            pltpu.VMEM((2, tile, V_DIM), F32),
        ],
        compiler_params=_params(("arbitrary", "arbitrary", "arbitrary")),
        name="prompt_attention",
    )(lamv, subln_g.reshape(1, V_DIM), q, k, v)


def _decode_attn_kernel(pt_ref, lamv_ref, g_ref, q_ref, kn_ref, vn_ref, *refs, n_pages_step, n_new, lam_init):
    k_refs = refs[:n_pages_step]
    v_refs = refs[n_pages_step:2 * n_pages_step]
    o_ref, m_ref, l_ref, acc_ref = refs[2 * n_pages_step:]
    p_id = pl.program_id(1)
    qbd = q_ref[0]
    n_rows = qbd.shape[0]
    rows_per_vh = 2 * SAMPLE_ROWS

    def update(kt, v, visible=None):
        n_chunks = kt.shape[1] // LANES
        s = _dot(qbd, kt.astype(BF16))
        if visible is not None:
            s = jnp.where(visible, s, NEG)
        chunks = [s[:, c * LANES:(c + 1) * LANES] for c in range(n_chunks)]
        m_old = m_ref[...]
        m_new = jnp.maximum(m_old, jnp.max(functools.reduce(jnp.maximum, chunks), axis=1, keepdims=True))
        alpha = jnp.exp2(m_old - m_new)
        ps = [jnp.exp2(c - m_new) for c in chunks]
        l_ref[...] = alpha * l_ref[...] + functools.reduce(lambda a, b: a + b, ps)
        m_ref[...] = m_new
        p = jnp.concatenate(ps, axis=1).astype(BF16)
        vb = v.astype(BF16)
        for vh in range(N_HEADS):
            rows = slice(vh * rows_per_vh, (vh + 1) * rows_per_vh)
            pv = _dot(p[rows, :], vb[:, vh * V_DIM:(vh + 1) * V_DIM])
            acc_ref[rows, :] = alpha[rows, :] * acc_ref[rows, :] + pv

    @pl.when(p_id == 0)
    def _():
        m_ref[...] = jnp.full(m_ref.shape, NEG, F32)
        l_ref[...] = jnp.zeros(l_ref.shape, F32)
        acc_ref[...] = jnp.zeros(acc_ref.shape, F32)
        shape = (n_rows, kn_ref.shape[2])
        r = lax.broadcasted_iota(jnp.int32, shape, 0) % SAMPLE_ROWS
        j = lax.broadcasted_iota(jnp.int32, shape, 1)
        update(kn_ref[0], vn_ref[0], (j <= r) & (j < n_new))

    d = qbd.shape[1]
    update(jnp.concatenate([r[...].reshape(d, r.shape[2]) for r in k_refs], axis=1),
           jnp.concatenate([pltpu.einshape("phd->p(hd)", r[...]) for r in v_refs], axis=0))

    @pl.when(p_id == pl.num_programs(1) - 1)
    def _():
        lam = _lambda_value(lamv_ref[...], lam_init)
        linv = 1.0 / jnp.sum(l_ref[...], axis=1, keepdims=True)
        g = g_ref[...]
        for vh in range(N_HEADS):
            r1 = slice((2 * vh) * SAMPLE_ROWS, (2 * vh + 1) * SAMPLE_ROWS)
            r2 = slice((2 * vh + 1) * SAMPLE_ROWS, (2 * vh + 2) * SAMPLE_ROWS)
            o1 = acc_ref[r1, :] * linv[r1, :]
            o2 = acc_ref[r2, :] * linv[r2, :]
            o_ref[0, :, vh * V_DIM:(vh + 1) * V_DIM] = _head_norm(o1 - lam * o2, g, lam_init)


def _decode_attention(page_table, q2, k_new, v_new, cache_k, cache_v, layer, lamv, subln_g, lam_init, n_new):
    db, n_rows, d = q2.shape
    n_pages = page_table.shape[1]
    nps = math.gcd(PAGES_PER_STEP, n_pages)
    steps = n_pages // nps
    cache_k = jnp.transpose(cache_k, (0, 1, 3, 4, 2))

    def page_spec(cache, j):
        return pl.BlockSpec((None, None) + cache.shape[2:], lambda b, p, pt: (layer, pt[b, p * nps + j], 0, 0, 0))

    per_seq = lambda a: pl.BlockSpec((1,) + a.shape[1:], lambda b, p, pt: (b,) + (0,) * (a.ndim - 1))
    grid_spec = pltpu.PrefetchScalarGridSpec(
        num_scalar_prefetch=1,
        grid=(db, steps),
        in_specs=[
            pl.BlockSpec((SUBLANES, LANES), lambda b, p, pt: (0, 0)),
            pl.BlockSpec((1, V_DIM), lambda b, p, pt: (0, 0)),
            per_seq(q2), per_seq(k_new), per_seq(v_new),
        ] + [page_spec(cache_k, j) for j in range(nps)] + [page_spec(cache_v, j) for j in range(nps)],
        out_specs=pl.BlockSpec((1, SAMPLE_ROWS, d), lambda b, p, pt: (b, 0, 0)),
        scratch_shapes=[
            pltpu.VMEM((n_rows, LANES), F32),
            pltpu.VMEM((n_rows, LANES), F32),
            pltpu.VMEM((n_rows, V_DIM), F32),
        ],
    )
    return pl.pallas_call(
        functools.partial(_decode_attn_kernel, n_pages_step=nps, n_new=n_new, lam_init=lam_init),
        grid_spec=grid_spec,
        out_shape=jax.ShapeDtypeStruct((db, SAMPLE_ROWS, d), F32),
        compiler_params=_params(("arbitrary", "arbitrary")),
        name="decode_attention",
    )(page_table, lamv, subln_g.reshape(1, V_DIM), q2, k_new, v_new,
      *([cache_k] * nps), *([cache_v] * nps))


def _memory_kv_kernel(mem_ref, g_ref, wk_ref, wv_ref, kf_ref, vf_ref, kb_ref, vb_ref):
    h = _rms(mem_ref[0], g_ref[...], EPS).astype(BF16)
    k = _dot(h, wk_ref[...])
    v = _dot(h, wv_ref[...])
    kf_ref[0] = k
    vf_ref[0] = v
    kb_ref[0] = k.astype(BF16)
    vb_ref[0] = v.astype(BF16)


def _memory_kv(mem, ln_g, w_mk_b, w_mv_b):
    b, m, d = mem.shape
    spec = pl.BlockSpec((1, m, d), lambda i: (i, 0, 0))
    return pl.pallas_call(
        _memory_kv_kernel,
        grid=(b,),
        in_specs=[spec, _const_spec((1, d)), _const_spec((d, d)), _const_spec((d, d))],
        out_specs=[spec] * 4,
        out_shape=[jax.ShapeDtypeStruct((b, m, d), F32)] * 2 + [jax.ShapeDtypeStruct((b, m, d), BF16)] * 2,
        compiler_params=_params(("arbitrary",)),
        name="memory_kv",
    )(mem, ln_g.reshape(1, d), w_mk_b, w_mv_b)


def _merge_vals(x, o, bc, sa, sc, wpa, wpc, wout):
    mixed = sa.astype(F32) * _dot(o.astype(BF16), wpa) + sc.astype(F32) * _dot(bc, wpc)
    return x + _dot(mixed.astype(BF16), wout)


def _memory_attn_vals(qm, mk, mv):
    d = qm.shape[1]
    hd = d // MEM_HEADS
    outs = []
    for h in range(MEM_HEADS):
        sl = slice(h * hd, (h + 1) * hd)
        s = _dot_nt(qm[:, sl].astype(BF16), mk[:, sl]) * (hd ** -0.5)
        e = jnp.exp(s - jnp.max(s, axis=1, keepdims=True))
        p = e / jnp.sum(e, axis=1, keepdims=True)
        outs.append(_dot(p.astype(BF16), mv[:, sl]))
    return jnp.concatenate(outs, axis=1)


def _router_vals(h, wr_hi, wr_lo, br):
    h_hi = h.astype(BF16)
    h_lo = (h - h_hi.astype(F32)).astype(BF16)
    lg = _dot(h_hi, wr_hi) + (_dot(h_lo, wr_hi) + _dot(h_hi, wr_lo)) + br
    lane = lax.broadcasted_iota(jnp.int32, lg.shape, 1)
    big = jnp.int32(4 * LANES)
    is_g = (lane >= N_EXPERTS) & (lane < N_EXPERTS + N_GROUPS)
    gl = jnp.where(is_g, lg, NEG)
    gmax = jnp.max(gl, axis=1, keepdims=True)
    gidx = jnp.min(jnp.where(is_g & (gl == gmax), lane - N_EXPERTS, big), axis=1, keepdims=True)
    gprob = 1.0 / jnp.sum(jnp.where(is_g, jnp.exp(gl - gmax), 0.0), axis=1, keepdims=True)
    in_group = (lane < N_EXPERTS) & ((lane // EXPERTS_PER_GROUP) == gidx)
    el = jnp.where(in_group, lg, NEG)
    l1 = jnp.max(el, axis=1, keepdims=True)
    i1 = jnp.min(jnp.where(in_group & (el == l1), lane, big), axis=1, keepdims=True)
    rest = in_group & (lane != i1)
    el2 = jnp.where(rest, lg, NEG)
    l2 = jnp.max(el2, axis=1, keepdims=True)
    i2 = jnp.min(jnp.where(rest & (el2 == l2), lane, big), axis=1, keepdims=True)
    e21 = jnp.exp(l2 - l1)
    p1 = 1.0 / (1.0 + e21)
    g1 = gprob * p1
    g2 = gprob * (e21 * p1)
    return jnp.where(lane == 0, i1.astype(F32),
                     jnp.where(lane == 1, i2.astype(F32),
                               jnp.where(lane == 2, g1, jnp.where(lane == 3, g2, 0.0))))


def _post_vals(x1, om, wmo, gffn, wr_hi, wr_lo, br):
    x2 = x1 + _dot(om.astype(BF16), wmo)
    h = _rms(x2, gffn, EPS)
    return x2, h, _router_vals(h, wr_hi, wr_lo, br)


def _prompt_merge_kernel(x_ref, o_ref, bc_ref, sa_ref, sc_ref, mk_ref, mv_ref,
                         wpa_ref, wpc_ref, wout_ref, gmem_ref, wmq_ref, wmo_ref,
                         gffn_ref, wrh_ref, wrl_ref, br_ref,
                         x2_ref, h_ref, ri_ref):
    x1 = _merge_vals(x_ref[...], o_ref[...], bc_ref[...], sa_ref[...], sc_ref[...],
                     wpa_ref[...], wpc_ref[...], wout_ref[...])
    qm = _dot(_rms(x1, gmem_ref[...], EPS).astype(BF16), wmq_ref[...])
    om = _memory_attn_vals(qm, mk_ref[0], mv_ref[0])
    x2, h, ri = _post_vals(x1, om, wmo_ref[...], gffn_ref[...], wrh_ref[...], wrl_ref[...], br_ref[...])
    x2_ref[...] = x2
    h_ref[...] = h
    ri_ref[...] = ri


def _prompt_merge(x2d, o, bc, sa, sc, mk_b, mv_b, wts, *, seq_len):
    t, d = x2d.shape
    tm = min(MIX_ROWS, seq_len)
    assert seq_len % tm == 0
    seq_tiles = seq_len // tm
    n_mem = mk_b.shape[1]
    row = pl.BlockSpec((tm, d), lambda i: (i, 0))
    mem = pl.BlockSpec((1, n_mem, d), lambda i: (i // seq_tiles, 0, 0))
    return pl.pallas_call(
        _prompt_merge_kernel,
        grid=(t // tm,),
        in_specs=[row] * 5 + [mem] * 2 + [_const_spec(w.shape) for w in wts],
        out_specs=[row, row, pl.BlockSpec((tm, LANES), lambda i: (i, 0))],
        out_shape=[jax.ShapeDtypeStruct((t, d), F32), jax.ShapeDtypeStruct((t, d), F32),
                   jax.ShapeDtypeStruct((t, LANES), F32)],
        compiler_params=_params(("arbitrary",)),
        name="prompt_merge",
    )(x2d, o, bc, sa, sc, mk_b, mv_b, *wts)


def _sample_merge_kernel(x_ref, o_ref, bc_ref, sa_ref, sc_ref, wpa_ref, wpc_ref, wout_ref, gmem_ref, wmq_ref,
                         x1_ref, qm_ref):
    x1 = _merge_vals(x_ref[...], o_ref[...], bc_ref[...], sa_ref[...], sc_ref[...],
                     wpa_ref[...], wpc_ref[...], wout_ref[...])
    x1_ref[...] = x1
    qm_ref[...] = _dot(_rms(x1, gmem_ref[...], EPS).astype(BF16), wmq_ref[...])


def _sample_memattn_kernel(qm_ref, mk_ref, mv_ref, om_ref):
    om_ref[0] = _memory_attn_vals(qm_ref[0], mk_ref[0].astype(BF16), mv_ref[0].astype(BF16))


def _sample_post_kernel(x1_ref, om_ref, wmo_ref, gffn_ref, wrh_ref, wrl_ref, br_ref, x2_ref, h_ref, ri_ref):
    x2, h, ri = _post_vals(x1_ref[...], om_ref[...], wmo_ref[...], gffn_ref[...],
                           wrh_ref[...], wrl_ref[...], br_ref[...])
    x2_ref[...] = x2
    h_ref[...] = h
    ri_ref[...] = ri


def _sample_merge(x2d, o, bc, sa, sc, mem_k, mem_v, wts, *, n_seq):
    t, d = x2d.shape
    wpa, wpc, wout, gmem, wmq, wmo, gffn, wrh, wrl, br = wts
    full = pl.BlockSpec((t, d), lambda i: (0, 0))
    x1, qm = pl.pallas_call(
        _sample_merge_kernel,
        grid=(1,),
        in_specs=[full] * 5 + [_const_spec(w.shape) for w in (wpa, wpc, wout, gmem, wmq)],
        out_specs=[full, full],
        out_shape=[jax.ShapeDtypeStruct((t, d), F32)] * 2,
        compiler_params=_params(("arbitrary",)),
        name="sample_merge",
    )(x2d, o, bc, sa, sc, wpa, wpc, wout, gmem, wmq)
    rows = t // n_seq
    n_mem = mem_k.shape[1]
    seq = pl.BlockSpec((1, rows, d), lambda i: (i, 0, 0))
    mem = pl.BlockSpec((1, n_mem, d), lambda i: (i, 0, 0))
    om = pl.pallas_call(
        _sample_memattn_kernel,
        grid=(n_seq,),
        in_specs=[seq, mem, mem],
        out_specs=seq,
        out_shape=jax.ShapeDtypeStruct((n_seq, rows, d), F32),
        compiler_params=_params(("arbitrary",)),
        name="sample_memattn",
    )(qm.reshape(n_seq, rows, d), mem_k, mem_v)
    return pl.pallas_call(
        _sample_post_kernel,
        grid=(1,),
        in_specs=[full, full] + [_const_spec(w.shape) for w in (wmo, gffn, wrh, wrl, br)],
        out_specs=[full, full, pl.BlockSpec((t, LANES), lambda i: (0, 0))],
        out_shape=[jax.ShapeDtypeStruct((t, d), F32), jax.ShapeDtypeStruct((t, d), F32),
                   jax.ShapeDtypeStruct((t, LANES), F32)],
        compiler_params=_params(("arbitrary",)),
        name="sample_post",
    )(x1, om.reshape(t, d), wmo, gffn, wrh, wrl, br)


def _row_copy(src_hbm, idx, dst, r, sem):
    return pltpu.make_async_copy(src_hbm.at[pl.ds(idx, 1), :], dst.at[pl.ds(r, 1), :], sem)


def _moe_kernel(blk_e_ref, nused_ref, tok_ref, tok_next_ref, h_hbm, w1_ref, w3_ref, w2_ref, ys_ref, xbuf, sem):
    i = pl.program_id(0)
    nused = nused_ref[0]

    def gather(toks, slot):
        for r in range(MOE_BLOCK):
            _row_copy(h_hbm, toks[0, 0, r], xbuf.at[slot], r, sem.at[slot]).start()

    @pl.when(i == 0)
    def _():
        gather(tok_ref, 0)

    @pl.when(i + 1 < nused)
    def _():
        gather(tok_next_ref, (i + 1) % 2)

    @pl.when(i < nused)
    def _():
        slot = i % 2
        for r in range(MOE_BLOCK):
            _row_copy(h_hbm, 0, xbuf.at[slot], r, sem.at[slot]).wait()
        x = xbuf[slot].astype(BF16)
        a = _dot(x, w1_ref[0].astype(BF16))
        b = _dot(x, w3_ref[0].astype(BF16))
        ys_ref[...] = _dot((jax.nn.silu(a) * b).astype(BF16), w2_ref[0].astype(BF16))

    @pl.when(i >= nused)
    def _():
        ys_ref[...] = jnp.zeros(ys_ref.shape, F32)


def _moe_experts(h, slot_tok, blk_e, nused, w1, w3, w2):
    t, d = h.shape
    n_blocks = blk_e.shape[0]
    de = w1.shape[2]
    toks = slot_tok.reshape(n_blocks, 1, MOE_BLOCK)
    tok_spec = lambda f: pl.BlockSpec((1, 1, MOE_BLOCK), f, memory_space=pltpu.SMEM)
    grid_spec = pltpu.PrefetchScalarGridSpec(
        num_scalar_prefetch=2,
        grid=(n_blocks,),
        in_specs=[
            tok_spec(lambda i, be, nu: (i, 0, 0)),
            tok_spec(lambda i, be, nu: (jnp.minimum(i + 1, n_blocks - 1), 0, 0)),
            pl.BlockSpec(memory_space=pl.ANY),
            pl.BlockSpec((1, d, de), lambda i, be, nu: (be[i], 0, 0)),
            pl.BlockSpec((1, d, de), lambda i, be, nu: (be[i], 0, 0)),
            pl.BlockSpec((1, de, d), lambda i, be, nu: (be[i], 0, 0)),
        ],
        out_specs=pl.BlockSpec((MOE_BLOCK, d), lambda i, be, nu: (i, 0)),
        scratch_shapes=[pltpu.VMEM((2, MOE_BLOCK, d), F32), pltpu.SemaphoreType.DMA((2,))],
    )
    return pl.pallas_call(
        _moe_kernel,
        grid_spec=grid_spec,
        out_shape=jax.ShapeDtypeStruct((n_blocks * MOE_BLOCK, d), F32),
        compiler_params=_params(("arbitrary",)),
        name="moe_experts",
    )(blk_e, nused, toks, toks, h, w1, w3, w2)


def _combine_kernel(slot_ref, slot_next_ref, x2_ref, ri_ref, g_ref, ys_hbm, y_ref, ybuf, sem, *, tm):
    i = pl.program_id(0)
    n = pl.num_programs(0)

    def gather(slots, buf):
        for r in range(2 * tm):
            _row_copy(ys_hbm, slots[0, 0, r], ybuf.at[buf], r, sem.at[buf]).start()

    @pl.when(i == 0)
    def _():
        gather(slot_ref, 0)

    @pl.when(i + 1 < n)
    def _():
        gather(slot_next_ref, (i + 1) % 2)

    buf = i % 2
    for r in range(2 * tm):
        _row_copy(ys_hbm, 0, ybuf.at[buf], r, sem.at[buf]).wait()
    ri = ri_ref[...]
    y = x2_ref[...] + ri[:, 2:3] * ybuf[buf, 0:tm, :] + ri[:, 3:4] * ybuf[buf, tm:2 * tm, :]
    y_ref[...] = _rms(y, g_ref[...], EPS)


def _moe_combine(x2, rinfo, slots, ys, ln_g):
    t, d = x2.shape
    tm = min(COMBINE_ROWS, t)
    assert t % tm == 0
    n = t // tm
    sl = slots.reshape(n, tm, 2).transpose(0, 2, 1).reshape(n, 1, 2 * tm)
    slot_spec = lambda f: pl.BlockSpec((1, 1, 2 * tm), f, memory_space=pltpu.SMEM)
    row = pl.BlockSpec((tm, d), lambda i: (i, 0))
    return pl.pallas_call(
        functools.partial(_combine_kernel, tm=tm),
        grid=(n,),
        in_specs=[
            slot_spec(lambda i: (i, 0, 0)),
            slot_spec(lambda i: (jnp.minimum(i + 1, n - 1), 0, 0)),
            row,
            pl.BlockSpec((tm, LANES), lambda i: (i, 0)),
            _const_spec((1, d)),
            pl.BlockSpec(memory_space=pl.ANY),
        ],
        out_specs=row,
        out_shape=jax.ShapeDtypeStruct((t, d), F32),
        scratch_shapes=[pltpu.VMEM((2, 2 * tm, d), F32), pltpu.SemaphoreType.DMA((2,))],
        compiler_params=_params(("arbitrary",)),
        name="moe_combine",
    )(sl, sl, x2, rinfo, ln_g.reshape(1, d), ys)


def _moe_plan(rinfo):
    t = rinfo.shape[0]
    eid = rinfo[:, :EXPERT_TOP_K].astype(jnp.int32).reshape(-1)
    n_assign = t * EXPERT_TOP_K
    experts = jnp.arange(N_EXPERTS, dtype=jnp.int32)
    order = jnp.argsort(eid, stable=True)
    pos = jnp.argsort(order)
    counts = jnp.sum((eid[:, None] == experts[None, :]).astype(jnp.int32), axis=0)
    starts = jnp.cumsum(counts) - counts
    padded = (counts + MOE_BLOCK - 1) // MOE_BLOCK * MOE_BLOCK
    pends = jnp.cumsum(padded)
    pstarts = pends - padded
    slots = (pos + (pstarts - starts)[eid]).reshape(t, EXPERT_TOP_K)
    n_blocks = -(-n_assign // MOE_BLOCK) + N_EXPERTS
    blk_start = jnp.arange(n_blocks, dtype=jnp.int32) * MOE_BLOCK
    blk_e = jnp.minimum(jnp.sum((pends[None, :] <= blk_start[:, None]).astype(jnp.int32), axis=1), N_EXPERTS - 1)
    j = (blk_start - pstarts[blk_e])[:, None] + jnp.arange(MOE_BLOCK, dtype=jnp.int32)[None, :]
    valid = (j >= 0) & (j < counts[blk_e][:, None])
    src = jnp.clip(starts[blk_e][:, None] + j, 0, n_assign - 1)
    slot_tok = jnp.where(valid, order[src] // EXPERT_TOP_K, 0).astype(jnp.int32).reshape(-1)
    nused = (pends[-1:] // MOE_BLOCK).astype(jnp.int32)
    return slots.astype(jnp.int32), slot_tok, blk_e.astype(jnp.int32), nused


def _moe_and_final(x2, h, rinfo, w1, w3, w2, ln_final_g):
    slots, slot_tok, blk_e, nused = _moe_plan(rinfo)
    ys = _moe_experts(h, slot_tok, blk_e, nused, w1, w3, w2)
    return _moe_combine(x2, rinfo, slots, ys, ln_final_g)


def _rope_tables(pos):
    half = HEAD_DIM // 2
    inv = 1.0 / (ROPE_THETA ** (jnp.arange(half, dtype=F32) / half))
    ang = pos.astype(F32)[:, None] * inv[None, :]
    cos = jnp.cos(ang)
    sin = jnp.sin(ang)
    cos_t = jnp.tile(jnp.concatenate([cos, cos], axis=1), (1, LANES // HEAD_DIM))
    sin_t = jnp.tile(jnp.concatenate([-sin, sin], axis=1), (1, LANES // HEAD_DIM))
    return cos_t, sin_t


def _lambda_init(layer_idx):
    return 0.8 - 0.6 * math.exp(-0.3 * layer_idx)


def _split_hi_lo(w):
    hi = w.astype(BF16)
    return hi, (w - hi.astype(F32)).astype(BF16)


def kernel(x_prompt, x_sample, cache_k, cache_v, state_conv, cache_mem_k, cache_mem_v, page_table, mem_prompt, ln_mix_g, w_in, lambda_q1, lambda_k1, lambda_q2, lambda_k2, subln_g, conv_w, w_proj_attn, w_proj_conv, w_out, ln_mem_g, ln_memkv_g, w_mq, w_mk, w_mv, w_mo, ln_ffn_g, w_group, b_group, w_router, b_router, w1, w3, w2, ln_final_g):
    b, s, d = x_prompt.shape
    db, dsq, _ = x_sample.shape
    depth = w_in.shape[0]
    n_pool, page = cache_k.shape[1], cache_k.shape[2]
    past = page_table.shape[1] * page
    assert d == 2 * N_HEADS * HEAD_DIM and w_in.shape[2] == 8 * d
    assert depth == 1, "only the last layer feeds the final norm; deeper stacks are not supported"
    assert CONV_WIDTH - 1 <= dsq <= SAMPLE_ROWS <= page
    pad = SAMPLE_ROWS - dsq

    cos_p, sin_p = _rope_tables(jnp.arange(s, dtype=jnp.int32))
    pos_s = past + jnp.minimum(jnp.arange(SAMPLE_ROWS, dtype=jnp.int32), dsq - 1)
    cos_s, sin_s = (jnp.tile(a, (db, 1)) for a in _rope_tables(pos_s))

    xp = x_prompt.reshape(b * s, d)
    xs = jnp.pad(x_sample, ((0, 0), (0, pad), (0, 0))).reshape(db * SAMPLE_ROWS, d)
    outs_p, outs_s = [], []
    for l in range(depth):
        lam_init = _lambda_init(l)
        lamv = jnp.pad(jnp.stack([lambda_q1[l], lambda_k1[l], lambda_q2[l], lambda_k2[l]]),
                       ((0, SUBLANES - 4), (0, LANES - HEAD_DIM)))
        w_in_b = w_in[l].astype(BF16)
        wr = jnp.pad(jnp.concatenate([w_router[l], w_group[l]], axis=1), ((0, 0), (0, LANES - N_EXPERTS - N_GROUPS)))
        br = jnp.pad(jnp.concatenate([b_router[l], b_group[l]]), (0, LANES - N_EXPERTS - N_GROUPS)).reshape(1, LANES)
        wr_hi, wr_lo = _split_hi_lo(wr)
        wts = (w_proj_attn[l].astype(BF16), w_proj_conv[l].astype(BF16), w_out[l].astype(BF16),
               ln_mem_g[l].reshape(1, d), w_mq[l].astype(BF16), w_mo[l].astype(BF16),
               ln_ffn_g[l].reshape(1, d), wr_hi, wr_lo, br)

        q, kf, kb, vf, vb, bc, sa, sc, tail = _mixer_in(xp, ln_mix_g[l], w_in_b, cos_p, sin_p, conv_w[l], seq_len=s)
        o = _prompt_attention(q.reshape(b, s, d), kb.reshape(b, s, d), vb.reshape(b, s, d), lamv, subln_g[l], lam_init)
        mkf, mvf, mkb, mvb = _memory_kv(mem_prompt, ln_memkv_g[l], w_mk[l].astype(BF16), w_mv[l].astype(BF16))
        x2, h, rinfo = _prompt_merge(xp, o.reshape(b * s, d), bc, sa, sc, mkb, mvb, wts, seq_len=s)
        nk_p = kf.reshape(b, 2 * N_HEADS, HEAD_DIM, s).transpose(0, 3, 1, 2)
        nv_p = vf.reshape(b, s, N_HEADS, V_DIM)
        nc_p = tail[:, SUBLANES - (CONV_WIDTH - 1):, :]
        nmk_p = mkf.reshape(b, -1, MEM_HEADS, d // MEM_HEADS)
        nmv_p = mvf.reshape(b, -1, MEM_HEADS, d // MEM_HEADS)
        xp_moe = (x2, h, rinfo)

        st = state_conv[l]
        zrow = jnp.zeros((db, SAMPLE_ROWS - 1, d), F32)
        s1 = jnp.concatenate([st[:, 1:2], zrow], axis=1).reshape(db * SAMPLE_ROWS, d)
        s2 = jnp.concatenate([st, zrow[:, 1:]], axis=1).reshape(db * SAMPLE_ROWS, d)
        q, kf, kb, vf, vb, bc, sa, sc, u = _mixer_in(xs, ln_mix_g[l], w_in_b, cos_s, sin_s, conv_w[l],
                                                      seq_len=SAMPLE_ROWS, state_rows=(s1, s2))
        q4 = q.reshape(db, SAMPLE_ROWS, 2 * N_HEADS, HEAD_DIM)
        eye = jnp.eye(2 * N_HEADS, dtype=BF16)
        q2 = jnp.einsum('brhd,hg->bhrgd', q4, eye).reshape(db, 2 * N_HEADS * SAMPLE_ROWS, d)
        new_page = lambda a: jnp.pad(a.reshape(db, SAMPLE_ROWS, d), ((0, 0), (0, page - SAMPLE_ROWS), (0, 0)))
        o = _decode_attention(page_table, q2, new_page(kf).transpose(0, 2, 1), new_page(vf), cache_k, cache_v, l,
                              lamv, subln_g[l], lam_init, dsq)
        x2, h, rinfo = _sample_merge(xs, o.reshape(db * SAMPLE_ROWS, d), bc, sa, sc,
                                     cache_mem_k[l].reshape(db, -1, d), cache_mem_v[l].reshape(db, -1, d),
                                     wts, n_seq=db)
        real = lambda a: a.reshape(db, SAMPLE_ROWS, -1)[:, :dsq].reshape(db * dsq, -1)
        nk_s = real(kf).reshape(db, dsq, 2 * N_HEADS, HEAD_DIM)
        nv_s = real(vf).reshape(db, dsq, N_HEADS, V_DIM)
        nc_s = u.reshape(db, SAMPLE_ROWS, d)[:, dsq - (CONV_WIDTH - 1):dsq]
        xs_moe = (real(x2), real(h), real(rinfo))

        outs_p.append((nk_p, nv_p, nc_p, nmk_p, nmv_p))
        outs_s.append((nk_s, nv_s, nc_s))

    y_prompt = _moe_and_final(*xp_moe, w1[depth - 1], w3[depth - 1], w2[depth - 1], ln_final_g).reshape(b, s, d)
    y_sample = _moe_and_final(*xs_moe, w1[depth - 1], w3[depth - 1], w2[depth - 1], ln_final_g).reshape(db, dsq, d)
    stack = lambda items, j: jnp.stack([it[j] for it in items])
    return (y_prompt, y_sample,
            stack(outs_p, 0), stack(outs_p, 1), stack(outs_p, 2), stack(outs_p, 3), stack(outs_p, 4),
            stack(outs_s, 0), stack(outs_s, 1), stack(outs_s, 2))
```

```python
import functools
import math

import jax
import jax.numpy as jnp
from jax import lax
from jax.experimental import pallas as pl
from jax.experimental.pallas import tpu as pltpu

N_HEADS = 8
HEAD_DIM = 64
V_DIM = 2 * HEAD_DIM
ROPE_THETA = 10000.0
CONV_WIDTH = 3
MEM_HEADS = 4
N_GROUPS = 4
EXPERTS_PER_GROUP = 8
N_EXPERTS = N_GROUPS * EXPERTS_PER_GROUP
EXPERT_TOP_K = 2
MOE_BLOCK = 128
EPS = 1e-6
SUBLN_EPS = 1e-5

LANES = 128
SUBLANES = 8
VMEM_LIMIT_BYTES = 56 * 1024 * 1024

MIX_ROWS = 512
ATTN_TILE = 1024
PAGES_PER_STEP = 8
SAMPLE_ROWS = 8
COMBINE_ROWS = 128
MOE_BUFFERS = 3

NEG = -1e30
Q_SCALE = HEAD_DIM ** -0.5 * math.log2(math.e)
F32 = jnp.float32
BF16 = jnp.bfloat16


def _params(sem):
    return pltpu.CompilerParams(dimension_semantics=sem, vmem_limit_bytes=VMEM_LIMIT_BYTES)


def _rms(x, g, eps):
    return x * lax.rsqrt(jnp.mean(x * x, axis=-1, keepdims=True) + eps) * g


def _dot(a, b):
    return jnp.dot(a, b, preferred_element_type=F32)


def _dot_nt(a, b):
    return lax.dot_general(a, b, (((1,), (1,)), ((), ())), preferred_element_type=F32)


def _const_spec(shape):
    nd = len(shape)
    return pl.BlockSpec(shape, lambda *_: (0,) * nd, pipeline_mode=pl.Buffered(1))


def _lambda_value(lamv, lam_init):
    a = jnp.sum(lamv[0:1, :] * lamv[1:2, :], axis=1, keepdims=True)
    b = jnp.sum(lamv[2:3, :] * lamv[3:4, :], axis=1, keepdims=True)
    return jnp.exp(a) - jnp.exp(b) + lam_init


def _head_norm(o, g, lam_init):
    return _rms(o, g, SUBLN_EPS) * (1.0 - lam_init)


def _mixer_in_kernel(*refs, tm, d, seq_tiles, period):
    if period:
        (x_ref, g_ref, w_ref, cos_ref, sin_ref, cw_ref, s1_ref, s2_ref,
         q_ref, kf_ref, kb_ref, vf_ref, vb_ref, bc_ref, sa_ref, sc_ref, tail_ref, ubuf_ref) = refs
    else:
        (x_ref, g_ref, w_ref, cos_ref, sin_ref, cw_ref,
         q_ref, kf_ref, kb_ref, vf_ref, vb_ref, bc_ref, sa_ref, sc_ref, tail_ref, ubuf_ref) = refs
    i = pl.program_id(0)
    h = _rms(x_ref[...], g_ref[...], EPS).astype(BF16)

    def proj(j):
        return _dot(h, w_ref[:, j * d:(j + 1) * d])

    cos = cos_ref[...]
    sin = sin_ref[...]
    lane = lax.broadcasted_iota(jnp.int32, (tm, LANES), 1)
    first_half = (lane % HEAD_DIM) < (HEAD_DIM // 2)

    def rope_chunks(z):
        for c in range(d // LANES):
            zc = z[:, c * LANES:(c + 1) * LANES]
            zr = jnp.where(first_half, pltpu.roll(zc, LANES - HEAD_DIM // 2, 1), pltpu.roll(zc, HEAD_DIM // 2, 1))
            yield c, zc * cos + zr * sin

    zq = proj(0)
    for c, y in rope_chunks(zq):
        q_ref[:, c * LANES:(c + 1) * LANES] = (y * Q_SCALE).astype(BF16)
    zk = proj(1)
    for c, y in rope_chunks(zk):
        if seq_tiles:
            kf_ref[0, c * LANES:(c + 1) * LANES, :] = y.T
        else:
            kf_ref[:, c * LANES:(c + 1) * LANES] = y
        kb_ref[:, c * LANES:(c + 1) * LANES] = y.astype(BF16)
    zv = proj(2)
    vf_ref[...] = zv
    vb_ref[...] = zv.astype(BF16)

    b_gate = proj(3)
    u = proj(4) * proj(5)
    @pl.when(i % max(seq_tiles, 1) == 0)
    def _():
        ubuf_ref[0:SUBLANES, :] = jnp.zeros((SUBLANES, d), F32)
    ubuf_ref[SUBLANES:SUBLANES + tm, :] = u
    p1 = ubuf_ref[SUBLANES - 1:SUBLANES - 1 + tm, :]
    p2 = ubuf_ref[SUBLANES - 2:SUBLANES - 2 + tm, :]
    if period:
        r = lax.broadcasted_iota(jnp.int32, (tm, 1), 0) % period
        p1 = jnp.where(r == 0, s1_ref[...], p1)
        p2 = jnp.where(r < 2, s2_ref[...], p2)
    cw = cw_ref[...]
    y_conv = p2 * cw[0:1, :] + p1 * cw[1:2, :] + u * cw[2:3, :]
    bc_ref[...] = (b_gate * y_conv).astype(BF16)
    if seq_tiles:
        last = ubuf_ref[tm:tm + SUBLANES, :]
        ubuf_ref[0:SUBLANES, :] = last
        tail_ref[0] = last
    else:
        tail_ref[...] = u

    sa_ref[...] = jax.nn.sigmoid(proj(6)).astype(BF16)
    sc_ref[...] = jax.nn.sigmoid(proj(7)).astype(BF16)


def _mixer_in(x2d, ln_g, w_in_b, cos_t, sin_t, conv_w, *, seq_len, state_rows=None):
    t, d = x2d.shape
    tm = min(MIX_ROWS, t)
    assert t % tm == 0
    n_tiles = t // tm
    if state_rows is None:
        assert seq_len % tm == 0
        seq_tiles, period = seq_len // tm, 0
        n_seq = t // seq_len
        pos_tiles = seq_tiles
    else:
        assert tm % seq_len == 0 and n_tiles == 1
        seq_tiles, period = 0, seq_len
        pos_tiles = 1
    row = lambda i: (i, 0)
    pos = lambda i: (i % pos_tiles, 0)
    in_specs = [
        pl.BlockSpec((tm, d), row),
        _const_spec((1, d)),
        _const_spec(w_in_b.shape),
        pl.BlockSpec((tm, LANES), pos),
        pl.BlockSpec((tm, LANES), pos),
        _const_spec((SUBLANES, d)),
    ]
    args = [x2d, ln_g.reshape(1, d), w_in_b, cos_t, sin_t, jnp.pad(conv_w, ((0, SUBLANES - CONV_WIDTH), (0, 0)))]
    bspec = pl.BlockSpec((tm, d), row)
    sd = lambda dt: jax.ShapeDtypeStruct((t, d), dt)
    if period:
        in_specs += [pl.BlockSpec((tm, d), row), pl.BlockSpec((tm, d), row)]
        args += list(state_rows)
        tail_shape = jax.ShapeDtypeStruct((t, d), F32)
        tail_spec = pl.BlockSpec((tm, d), row)
        kf_shape, kf_spec = sd(F32), bspec
    else:
        tail_shape = jax.ShapeDtypeStruct((n_seq, SUBLANES, d), F32)
        tail_spec = pl.BlockSpec((1, SUBLANES, d), lambda i: (i // seq_tiles, 0, 0))
        kf_shape = jax.ShapeDtypeStruct((n_seq, d, seq_len), F32)
        kf_spec = pl.BlockSpec((1, d, tm), lambda i: (i // seq_tiles, 0, i % seq_tiles))
    return pl.pallas_call(
        functools.partial(_mixer_in_kernel, tm=tm, d=d, seq_tiles=seq_tiles, period=period),
        grid=(n_tiles,),
        in_specs=in_specs,
        out_specs=[bspec, kf_spec] + [bspec] * 6 + [tail_spec],
        out_shape=[sd(BF16), kf_shape, sd(BF16), sd(F32), sd(BF16), sd(BF16), sd(BF16), sd(BF16), tail_shape],
        scratch_shapes=[pltpu.VMEM((tm + 2 * SUBLANES, d), F32)],
        compiler_params=_params(("arbitrary",)),
        name="mixer_in",
    )(*args)


def _prompt_attn_kernel(lamv_ref, g_ref, q_ref, k_ref, v_ref, o_ref,
                        s0_ref, s1_ref, mx0_ref, mx1_ref, m_ref, acc_ref, *, tile, lam_init):
    qi = pl.program_id(2)
    q = q_ref[0]
    lane = lax.broadcasted_iota(jnp.int32, q.shape, 1)
    zero = jnp.zeros_like(q)
    qs = (jnp.where(lane < HEAD_DIM, q, zero), jnp.where(lane >= HEAD_DIM, q, zero))
    m_ref[...] = jnp.full(m_ref.shape, NEG, F32)
    acc_ref[...] = jnp.zeros(acc_ref.shape, F32)
    n_chunks = tile // LANES

    s_bufs = (s0_ref, s1_ref)
    mx_bufs = (mx0_ref, mx1_ref)

    def scores(ki, buf, diagonal=False):
        off = pl.multiple_of(ki * tile, tile)
        k = k_ref[0, pl.ds(off, tile), :]
        for j in range(2):
            s = _dot_nt(qs[j], k)
            if diagonal:
                row = lax.broadcasted_iota(jnp.int32, (tile, tile), 0)
                col = lax.broadcasted_iota(jnp.int32, (tile, tile), 1)
                s = jnp.where(col <= row, s, NEG)
            s_bufs[buf][j] = s
            mx_bufs[buf][j] = functools.reduce(
                jnp.maximum, [s[:, c * LANES:(c + 1) * LANES] for c in range(n_chunks)])

    def absorb(ki, buf):
        off = pl.multiple_of(ki * tile, tile)
        v = v_ref[0, pl.ds(off, tile), :]
        v_ones = jnp.concatenate([v, jnp.ones_like(v)], axis=1)
        for j in range(2):
            m_old = m_ref[j]
            m_new = jnp.maximum(m_old, jnp.max(mx_bufs[buf][j], axis=1, keepdims=True))
            alpha = jnp.exp2(m_old - m_new)
            ps = [jnp.exp2(s_bufs[buf][j, :, c * LANES:(c + 1) * LANES] - m_new) for c in range(n_chunks)]
            p = jnp.concatenate(ps, axis=1).astype(BF16)
            acc_ref[j] = jnp.concatenate([alpha, alpha], axis=1) * acc_ref[j] + _dot(p, v_ones)
            m_ref[j] = m_new

    odd = qi % 2 == 1

    @pl.when(qi == 0)
    def _():
        scores(0, 0, diagonal=True)
        absorb(0, 0)

    @pl.when(qi > 0)
    def _():
        scores(0, 0)

    def pair(t, carry):
        i = 2 * t + 1
        scores(i, 1)
        absorb(i - 1, 0)
        scores(i + 1, 0)
        absorb(i, 1)
        return carry

    lax.fori_loop(0, (qi - 1) // 2, pair, 0)

    @pl.when(odd)
    def _():
        scores(qi, 1, diagonal=True)
        absorb(qi - 1, 0)
        absorb(qi, 1)

    @pl.when((qi > 0) & jnp.logical_not(odd))
    def _():
        scores(qi - 1, 1)
        absorb(qi - 2, 0)
        scores(qi, 0, diagonal=True)
        absorb(qi - 1, 1)
        absorb(qi, 0)

    lam = _lambda_value(lamv_ref[...], lam_init)
    o = acc_ref[0, :, :V_DIM] / acc_ref[0, :, V_DIM:] - lam * (acc_ref[1, :, :V_DIM] / acc_ref[1, :, V_DIM:])
    o_ref[0] = _head_norm(o, g_ref[...], lam_init).astype(BF16)


def _prompt_attention(q, k, v, lamv, subln_g, lam_init):
    b, s, d = q.shape
    tile = min(ATTN_TILE, s)
    assert s % tile == 0
    nq = s // tile
    return pl.pallas_call(
        functools.partial(_prompt_attn_kernel, tile=tile, lam_init=lam_init),
        grid=(b, N_HEADS, nq),
        in_specs=[
            _const_spec((SUBLANES, LANES)),
            _const_spec((1, V_DIM)),
            pl.BlockSpec((1, tile, LANES), lambda bi, h, qi: (bi, qi, h)),
            pl.BlockSpec((1, s, LANES), lambda bi, h, qi: (bi, 0, h)),
            pl.BlockSpec((1, s, LANES), lambda bi, h, qi: (bi, 0, h)),
        ],
        out_specs=pl.BlockSpec((1, tile, LANES), lambda bi, h, qi: (bi, qi, h)),
        out_shape=jax.ShapeDtypeStruct((b, s, d), BF16),
        scratch_shapes=[
            pltpu.VMEM((2, tile, tile), F32),
            pltpu.VMEM((2, tile, tile), F32),
            pltpu.VMEM((2, tile, LANES), F32),
            pltpu.VMEM((2, tile, LANES), F32),
            pltpu.VMEM((2, tile, LANES), F32),
            pltpu.VMEM((2, tile, 2 * V_DIM), F32),
        ],
        compiler_params=_params(("arbitrary", "arbitrary", "arbitrary")),
        name="prompt_attention",
    )(lamv, subln_g.reshape(1, V_DIM), q, k, v)


def _decode_attn_kernel(pt_ref, lamv_ref, g_ref, q_ref, kn_ref, vn_ref, *refs, n_pages_step, n_new, lam_init):
    k_refs = refs[:n_pages_step]
    v_refs = refs[n_pages_step:2 * n_pages_step]
    o_ref, m_ref, l_ref, acc_ref = refs[2 * n_pages_step:]
    p_id = pl.program_id(1)
    qbd = q_ref[0]
    n_rows = qbd.shape[0]
    rows_per_vh = 2 * SAMPLE_ROWS

    def update(kt, v, visible=None):
        n_chunks = kt.shape[1] // LANES
        s = _dot(qbd, kt.astype(BF16))
        if visible is not None:
            s = jnp.where(visible, s, NEG)
        chunks = [s[:, c * LANES:(c + 1) * LANES] for c in range(n_chunks)]
        m_old = m_ref[...]
        m_new = jnp.maximum(m_old, jnp.max(functools.reduce(jnp.maximum, chunks), axis=1, keepdims=True))
        alpha = jnp.exp2(m_old - m_new)
        ps = [jnp.exp2(c - m_new) for c in chunks]
        l_ref[...] = alpha * l_ref[...] + functools.reduce(lambda a, b: a + b, ps)
        m_ref[...] = m_new
        p = jnp.concatenate(ps, axis=1).astype(BF16)
        vb = v.astype(BF16)
        for vh in range(N_HEADS):
            rows = slice(vh * rows_per_vh, (vh + 1) * rows_per_vh)
            pv = _dot(p[rows, :], vb[:, vh * V_DIM:(vh + 1) * V_DIM])
            acc_ref[rows, :] = alpha[rows, :] * acc_ref[rows, :] + pv

    @pl.when(p_id == 0)
    def _():
        m_ref[...] = jnp.full(m_ref.shape, NEG, F32)
        l_ref[...] = jnp.zeros(l_ref.shape, F32)
        acc_ref[...] = jnp.zeros(acc_ref.shape, F32)
        shape = (n_rows, kn_ref.shape[2])
        r = lax.broadcasted_iota(jnp.int32, shape, 0) % SAMPLE_ROWS
        j = lax.broadcasted_iota(jnp.int32, shape, 1)
        update(kn_ref[0], vn_ref[0], (j <= r) & (j < n_new))

    d = qbd.shape[1]
    update(jnp.concatenate([r[...].reshape(d, r.shape[2]) for r in k_refs], axis=1),
           jnp.concatenate([pltpu.einshape("phd->p(hd)", r[...]) for r in v_refs], axis=0))

    @pl.when(p_id == pl.num_programs(1) - 1)
    def _():
        lam = _lambda_value(lamv_ref[...], lam_init)
        linv = 1.0 / jnp.sum(l_ref[...], axis=1, keepdims=True)
        g = g_ref[...]
        for vh in range(N_HEADS):
            r1 = slice((2 * vh) * SAMPLE_ROWS, (2 * vh + 1) * SAMPLE_ROWS)
            r2 = slice((2 * vh + 1) * SAMPLE_ROWS, (2 * vh + 2) * SAMPLE_ROWS)
            o1 = acc_ref[r1, :] * linv[r1, :]
            o2 = acc_ref[r2, :] * linv[r2, :]
            o_ref[0, :, vh * V_DIM:(vh + 1) * V_DIM] = _head_norm(o1 - lam * o2, g, lam_init)


def _decode_attention(page_table, q2, k_new, v_new, cache_k, cache_v, layer, lamv, subln_g, lam_init, n_new):
    db, n_rows, d = q2.shape
    n_pages = page_table.shape[1]
    nps = math.gcd(PAGES_PER_STEP, n_pages)
    steps = n_pages // nps
    cache_k = jnp.transpose(cache_k, (0, 1, 3, 4, 2))

    def page_spec(cache, j):
        return pl.BlockSpec((None, None) + cache.shape[2:], lambda b, p, pt: (layer, pt[b, p * nps + j], 0, 0, 0))

    per_seq = lambda a: pl.BlockSpec((1,) + a.shape[1:], lambda b, p, pt: (b,) + (0,) * (a.ndim - 1))
    grid_spec = pltpu.PrefetchScalarGridSpec(
        num_scalar_prefetch=1,
        grid=(db, steps),
        in_specs=[
            pl.BlockSpec((SUBLANES, LANES), lambda b, p, pt: (0, 0)),
            pl.BlockSpec((1, V_DIM), lambda b, p, pt: (0, 0)),
            per_seq(q2), per_seq(k_new), per_seq(v_new),
        ] + [page_spec(cache_k, j) for j in range(nps)] + [page_spec(cache_v, j) for j in range(nps)],
        out_specs=pl.BlockSpec((1, SAMPLE_ROWS, d), lambda b, p, pt: (b, 0, 0)),
        scratch_shapes=[
            pltpu.VMEM((n_rows, LANES), F32),
            pltpu.VMEM((n_rows, LANES), F32),
            pltpu.VMEM((n_rows, V_DIM), F32),
        ],
    )
    return pl.pallas_call(
        functools.partial(_decode_attn_kernel, n_pages_step=nps, n_new=n_new, lam_init=lam_init),
        grid_spec=grid_spec,
        out_shape=jax.ShapeDtypeStruct((db, SAMPLE_ROWS, d), F32),
        compiler_params=_params(("arbitrary", "arbitrary")),
        name="decode_attention",
    )(page_table, lamv, subln_g.reshape(1, V_DIM), q2, k_new, v_new,
      *([cache_k] * nps), *([cache_v] * nps))


def _memory_kv_kernel(mem_ref, g_ref, wk_ref, wv_ref, kf_ref, vf_ref, kb_ref, vb_ref):
    h = _rms(mem_ref[0], g_ref[...], EPS).astype(BF16)
    k = _dot(h, wk_ref[...])
    v = _dot(h, wv_ref[...])
    kf_ref[0] = k
    vf_ref[0] = v
    kb_ref[0] = k.astype(BF16)
    vb_ref[0] = v.astype(BF16)


def _memory_kv(mem, ln_g, w_mk_b, w_mv_b):
    b, m, d = mem.shape
    spec = pl.BlockSpec((1, m, d), lambda i: (i, 0, 0))
    return pl.pallas_call(
        _memory_kv_kernel,
        grid=(b,),
        in_specs=[spec, _const_spec((1, d)), _const_spec((d, d)), _const_spec((d, d))],
        out_specs=[spec] * 4,
        out_shape=[jax.ShapeDtypeStruct((b, m, d), F32)] * 2 + [jax.ShapeDtypeStruct((b, m, d), BF16)] * 2,
        compiler_params=_params(("arbitrary",)),
        name="memory_kv",
    )(mem, ln_g.reshape(1, d), w_mk_b, w_mv_b)


def _merge_vals(x, o, bc, sa, sc, wpa, wpc, wout):
    mixed = sa.astype(F32) * _dot(o.astype(BF16), wpa) + sc.astype(F32) * _dot(bc, wpc)
    return x + _dot(mixed.astype(BF16), wout)


def _memory_attn_vals(qm, mk, mv):
    d = qm.shape[1]
    hd = d // MEM_HEADS
    outs = []
    for h in range(MEM_HEADS):
        sl = slice(h * hd, (h + 1) * hd)
        s = _dot_nt(qm[:, sl].astype(BF16), mk[:, sl]) * (hd ** -0.5)
        e = jnp.exp(s - jnp.max(s, axis=1, keepdims=True))
        p = e / jnp.sum(e, axis=1, keepdims=True)
        outs.append(_dot(p.astype(BF16), mv[:, sl]))
    return jnp.concatenate(outs, axis=1)


def _router_vals(h, wr_hi, wr_lo, br):
    h_hi = h.astype(BF16)
    h_lo = (h - h_hi.astype(F32)).astype(BF16)
    lg = _dot(h_hi, wr_hi) + (_dot(h_lo, wr_hi) + _dot(h_hi, wr_lo)) + br
    lane = lax.broadcasted_iota(jnp.int32, lg.shape, 1)
    big = jnp.int32(4 * LANES)
    is_g = (lane >= N_EXPERTS) & (lane < N_EXPERTS + N_GROUPS)
    gl = jnp.where(is_g, lg, NEG)
    gmax = jnp.max(gl, axis=1, keepdims=True)
    gidx = jnp.min(jnp.where(is_g & (gl == gmax), lane - N_EXPERTS, big), axis=1, keepdims=True)
    gprob = 1.0 / jnp.sum(jnp.where(is_g, jnp.exp(gl - gmax), 0.0), axis=1, keepdims=True)
    in_group = (lane < N_EXPERTS) & ((lane // EXPERTS_PER_GROUP) == gidx)
    el = jnp.where(in_group, lg, NEG)
    l1 = jnp.max(el, axis=1, keepdims=True)
    i1 = jnp.min(jnp.where(in_group & (el == l1), lane, big), axis=1, keepdims=True)
    rest = in_group & (lane != i1)
    el2 = jnp.where(rest, lg, NEG)
    l2 = jnp.max(el2, axis=1, keepdims=True)
    i2 = jnp.min(jnp.where(rest & (el2 == l2), lane, big), axis=1, keepdims=True)
    e21 = jnp.exp(l2 - l1)
    p1 = 1.0 / (1.0 + e21)
    g1 = gprob * p1
    g2 = gprob * (e21 * p1)
    return jnp.where(lane == 0, i1.astype(F32),
                     jnp.where(lane == 1, i2.astype(F32),
                               jnp.where(lane == 2, g1, jnp.where(lane == 3, g2, 0.0))))


def _store_token_tiles(ref, x):
    for c in range(ref.shape[1]):
        ref[:, c, :] = x[:, c * LANES:(c + 1) * LANES]


def _post_vals(x1, om, wmo, gffn, wr_hi, wr_lo, br):
    x2 = x1 + _dot(om.astype(BF16), wmo)
    h = _rms(x2, gffn, EPS)
    return x2, h, _router_vals(h, wr_hi, wr_lo, br)


def _prompt_merge_kernel(x_ref, o_ref, bc_ref, sa_ref, sc_ref, mk_ref, mv_ref,
                         wpa_ref, wpc_ref, wout_ref, gmem_ref, wmq_ref, wmo_ref,
                         gffn_ref, wrh_ref, wrl_ref, br_ref,
                         x2_ref, h_ref, ri_ref):
    x1 = _merge_vals(x_ref[...], o_ref[...], bc_ref[...], sa_ref[...], sc_ref[...],
                     wpa_ref[...], wpc_ref[...], wout_ref[...])
    qm = _dot(_rms(x1, gmem_ref[...], EPS).astype(BF16), wmq_ref[...])
    om = _memory_attn_vals(qm, mk_ref[0], mv_ref[0])
    x2, h, ri = _post_vals(x1, om, wmo_ref[...], gffn_ref[...], wrh_ref[...], wrl_ref[...], br_ref[...])
    x2_ref[...] = x2
    _store_token_tiles(h_ref, h)
    ri_ref[...] = ri


def _prompt_merge(x2d, o, bc, sa, sc, mk_b, mv_b, wts, *, seq_len):
    t, d = x2d.shape
    tm = min(MIX_ROWS, seq_len)
    assert seq_len % tm == 0
    seq_tiles = seq_len // tm
    n_mem = mk_b.shape[1]
    row = pl.BlockSpec((tm, d), lambda i: (i, 0))
    mem = pl.BlockSpec((1, n_mem, d), lambda i: (i // seq_tiles, 0, 0))
    return pl.pallas_call(
        _prompt_merge_kernel,
        grid=(t // tm,),
        in_specs=[row] * 5 + [mem] * 2 + [_const_spec(w.shape) for w in wts],
        out_specs=[row, pl.BlockSpec((tm, d // LANES, LANES), lambda i: (i, 0, 0)),
                   pl.BlockSpec((tm, LANES), lambda i: (i, 0))],
        out_shape=[jax.ShapeDtypeStruct((t, d), F32), jax.ShapeDtypeStruct((t, d // LANES, LANES), F32),
                   jax.ShapeDtypeStruct((t, LANES), F32)],
        compiler_params=_params(("arbitrary",)),
        name="prompt_merge",
    )(x2d, o, bc, sa, sc, mk_b, mv_b, *wts)


def _sample_merge_kernel(x_ref, o_ref, bc_ref, sa_ref, sc_ref, wpa_ref, wpc_ref, wout_ref, gmem_ref, wmq_ref,
                         x1_ref, qm_ref):
    x1 = _merge_vals(x_ref[...], o_ref[...], bc_ref[...], sa_ref[...], sc_ref[...],
                     wpa_ref[...], wpc_ref[...], wout_ref[...])
    x1_ref[...] = x1
    qm_ref[...] = _dot(_rms(x1, gmem_ref[...], EPS).astype(BF16), wmq_ref[...])


def _sample_memattn_kernel(qm_ref, mk_ref, mv_ref, om_ref):
    om_ref[0] = _memory_attn_vals(qm_ref[0], mk_ref[0].astype(BF16), mv_ref[0].astype(BF16))


def _sample_post_kernel(x1_ref, om_ref, wmo_ref, gffn_ref, wrh_ref, wrl_ref, br_ref, x2_ref, h_ref, ri_ref):
    x2, h, ri = _post_vals(x1_ref[...], om_ref[...], wmo_ref[...], gffn_ref[...],
                           wrh_ref[...], wrl_ref[...], br_ref[...])
    x2_ref[...] = x2
    _store_token_tiles(h_ref, h)
    ri_ref[...] = ri


def _sample_merge(x2d, o, bc, sa, sc, mem_k, mem_v, wts, *, n_seq):
    t, d = x2d.shape
    wpa, wpc, wout, gmem, wmq, wmo, gffn, wrh, wrl, br = wts
    full = pl.BlockSpec((t, d), lambda i: (0, 0))
    x1, qm = pl.pallas_call(
        _sample_merge_kernel,
        grid=(1,),
        in_specs=[full] * 5 + [_const_spec(w.shape) for w in (wpa, wpc, wout, gmem, wmq)],
        out_specs=[full, full],
        out_shape=[jax.ShapeDtypeStruct((t, d), F32)] * 2,
        compiler_params=_params(("arbitrary",)),
        name="sample_merge",
    )(x2d, o, bc, sa, sc, wpa, wpc, wout, gmem, wmq)
    rows = t // n_seq
    n_mem = mem_k.shape[1]
    seq = pl.BlockSpec((1, rows, d), lambda i: (i, 0, 0))
    mem = pl.BlockSpec((1, n_mem, d), lambda i: (i, 0, 0))
    om = pl.pallas_call(
        _sample_memattn_kernel,
        grid=(n_seq,),
        in_specs=[seq, mem, mem],
        out_specs=seq,
        out_shape=jax.ShapeDtypeStruct((n_seq, rows, d), F32),
        compiler_params=_params(("arbitrary",)),
        name="sample_memattn",
    )(qm.reshape(n_seq, rows, d), mem_k, mem_v)
    return pl.pallas_call(
        _sample_post_kernel,
        grid=(1,),
        in_specs=[full, full] + [_const_spec(w.shape) for w in (wmo, gffn, wrh, wrl, br)],
        out_specs=[full, pl.BlockSpec((t, d // LANES, LANES), lambda i: (0, 0, 0)),
                   pl.BlockSpec((t, LANES), lambda i: (0, 0))],
        out_shape=[jax.ShapeDtypeStruct((t, d), F32), jax.ShapeDtypeStruct((t, d // LANES, LANES), F32),
                   jax.ShapeDtypeStruct((t, LANES), F32)],
        compiler_params=_params(("arbitrary",)),
        name="sample_post",
    )(x1, om.reshape(t, d), wmo, gffn, wrh, wrl, br)


def _row_copy(src_hbm, idx, dst, r, sem):
    return pltpu.make_async_copy(src_hbm.at[pl.ds(idx, 1), :], dst.at[pl.ds(r, 1), :], sem)


def _token_copy(src_hbm, idx, dst, r, sem):
    return pltpu.make_async_copy(src_hbm.at[idx], dst.at[r], sem)


def _moe_kernel(blk_e_ref, tok0_ref, tok1_ref, tok_next_ref, h_hbm, w1_ref, w3_ref, w2_ref, ys_ref,
                xbuf, w1b_ref, w3b_ref, w2b_ref, sem):
    i = pl.program_id(0)
    last = pl.num_programs(0) - 1
    cur, nxt = i % MOE_BUFFERS, (i + MOE_BUFFERS - 1) % MOE_BUFFERS
    d, de = w1_ref.shape[1], w1_ref.shape[2]

    def wait_rows(buf):
        for r in range(MOE_BLOCK):
            _token_copy(h_hbm, 0, xbuf.at[buf], r, sem.at[buf]).wait()

    @pl.when(i == 0)
    def _():
        for b, toks in enumerate((tok0_ref, tok1_ref)):
            for r in range(MOE_BLOCK):
                _token_copy(h_hbm, toks[0, 0, r], xbuf.at[b], r, sem.at[b]).start()

    @pl.when((i == 0) | (blk_e_ref[i] != blk_e_ref[jnp.maximum(i - 1, 0)]))
    def _():
        w1b_ref[...] = w1_ref[0].astype(BF16)
        w3b_ref[...] = w3_ref[0].astype(BF16)
        w2b_ref[...] = w2_ref[0].astype(BF16)

    wait_rows(cur)
    x = pltpu.einshape("tcl->t(cl)", xbuf[cur]).astype(BF16)
    rows = iter(range(MOE_BLOCK))
    n_chunks = 4
    per_chunk = MOE_BLOCK // (3 * n_chunks)

    def start_rows(n):
        for _ in range(n):
            r = next(rows)
            _token_copy(h_hbm, tok_next_ref[0, 0, r], xbuf.at[nxt], r, sem.at[nxt]).start(priority=r % 2)

    def up(w_ref):
        acc = None
        for c in range(n_chunks):
            start_rows(per_chunk)
            ks = slice(c * (d // n_chunks), (c + 1) * (d // n_chunks))
            part = _dot(x[:, ks], w_ref[ks, :])
            acc = part if acc is None else acc + part
        return acc

    a = up(w1b_ref)
    b = up(w3b_ref)
    hmid = (jax.nn.silu(a) * b).astype(BF16)
    for c in range(n_chunks):
        start_rows(per_chunk)
        ns = slice(c * (d // n_chunks), (c + 1) * (d // n_chunks))
        ys_ref[:, ns] = _dot(hmid, w2b_ref[:, ns])
    start_rows(MOE_BLOCK - 3 * n_chunks * per_chunk)

    @pl.when(i == last)
    def _():
        for b in range(1, MOE_BUFFERS):
            wait_rows((i + b) % MOE_BUFFERS)


def _moe_experts(h, slot_tok, blk_e, w1, w3, w2):
    d = h.shape[1] * h.shape[2]
    n_blocks = blk_e.shape[0]
    de = w1.shape[2]
    toks = slot_tok.reshape(n_blocks, 1, MOE_BLOCK)
    tok_spec = lambda f: pl.BlockSpec((1, 1, MOE_BLOCK), f, memory_space=pltpu.SMEM)
    grid_spec = pltpu.PrefetchScalarGridSpec(
        num_scalar_prefetch=1,
        grid=(n_blocks,),
        in_specs=[
            tok_spec(lambda i, be: (i, 0, 0)),
            tok_spec(lambda i, be: (jnp.minimum(i + 1, n_blocks - 1), 0, 0)),
            tok_spec(lambda i, be: (jnp.minimum(i + MOE_BUFFERS - 1, n_blocks - 1), 0, 0)),
            pl.BlockSpec(memory_space=pl.ANY),
            pl.BlockSpec((1, d, de), lambda i, be: (be[i], 0, 0)),
            pl.BlockSpec((1, d, de), lambda i, be: (be[i], 0, 0)),
            pl.BlockSpec((1, de, d), lambda i, be: (be[i], 0, 0)),
        ],
        out_specs=pl.BlockSpec((MOE_BLOCK, d), lambda i, be: (i, 0)),
        scratch_shapes=[
            pltpu.VMEM((MOE_BUFFERS, MOE_BLOCK, d // LANES, LANES), F32),
            pltpu.VMEM((d, de), BF16), pltpu.VMEM((d, de), BF16), pltpu.VMEM((de, d), BF16),
            pltpu.SemaphoreType.DMA((MOE_BUFFERS,)),
        ],
    )
    return pl.pallas_call(
        _moe_kernel,
        grid_spec=grid_spec,
        out_shape=jax.ShapeDtypeStruct((n_blocks * MOE_BLOCK, d), F32),
        compiler_params=_params(("arbitrary",)),
        name="moe_experts",
    )(blk_e, toks, toks, toks, h, w1, w3, w2)


def _combine_kernel(slot_ref, slot_next_ref, x2_ref, ri_ref, g_ref, ys_hbm, y_ref, ybuf, sem, *, tm):
    i = pl.program_id(0)
    n = pl.num_programs(0)

    def gather(slots, buf):
        for r in range(2 * tm):
            _row_copy(ys_hbm, slots[0, 0, r], ybuf.at[buf], r, sem.at[buf]).start()

    @pl.when(i == 0)
    def _():
        gather(slot_ref, 0)

    @pl.when(i + 1 < n)
    def _():
        gather(slot_next_ref, (i + 1) % 2)

    buf = i % 2
    for r in range(2 * tm):
        _row_copy(ys_hbm, 0, ybuf.at[buf], r, sem.at[buf]).wait()
    ri = ri_ref[...]
    y = x2_ref[...] + ri[:, 2:3] * ybuf[buf, 0:tm, :] + ri[:, 3:4] * ybuf[buf, tm:2 * tm, :]
    y_ref[...] = _rms(y, g_ref[...], EPS)


def _moe_combine(x2, rinfo, slots, ys, ln_g):
    t, d = x2.shape
    tm = min(COMBINE_ROWS, t)
    assert t % tm == 0
    n = t // tm
    sl = slots.reshape(n, tm, 2).transpose(0, 2, 1).reshape(n, 1, 2 * tm)
    slot_spec = lambda f: pl.BlockSpec((1, 1, 2 * tm), f, memory_space=pltpu.SMEM)
    row = pl.BlockSpec((tm, d), lambda i: (i, 0))
    return pl.pallas_call(
        functools.partial(_combine_kernel, tm=tm),
        grid=(n,),
        in_specs=[
            slot_spec(lambda i: (i, 0, 0)),
            slot_spec(lambda i: (jnp.minimum(i + 1, n - 1), 0, 0)),
            row,
            pl.BlockSpec((tm, LANES), lambda i: (i, 0)),
            _const_spec((1, d)),
            pl.BlockSpec(memory_space=pl.ANY),
        ],
        out_specs=row,
        out_shape=jax.ShapeDtypeStruct((t, d), F32),
        scratch_shapes=[pltpu.VMEM((2, 2 * tm, d), F32), pltpu.SemaphoreType.DMA((2,))],
        compiler_params=_params(("arbitrary",)),
        name="moe_combine",
    )(sl, sl, x2, rinfo, ln_g.reshape(1, d), ys)


def _moe_plan(rinfo):
    t = rinfo.shape[0]
    eid = rinfo[:, :EXPERT_TOP_K].astype(jnp.int32).reshape(-1)
    n_assign = t * EXPERT_TOP_K
    experts = jnp.arange(N_EXPERTS, dtype=jnp.int32)
    order = jnp.argsort(eid, stable=True)
    pos = jnp.argsort(order)
    counts = jnp.sum((eid[:, None] == experts[None, :]).astype(jnp.int32), axis=0)
    starts = jnp.cumsum(counts) - counts
    padded = (counts + MOE_BLOCK - 1) // MOE_BLOCK * MOE_BLOCK
    pends = jnp.cumsum(padded)
    pstarts = pends - padded
    slots = (pos + (pstarts - starts)[eid]).reshape(t, EXPERT_TOP_K)
    n_blocks = -(-n_assign // MOE_BLOCK) + N_EXPERTS
    blk_start = jnp.arange(n_blocks, dtype=jnp.int32) * MOE_BLOCK
    blk_e = jnp.minimum(jnp.sum((pends[None, :] <= blk_start[:, None]).astype(jnp.int32), axis=1), N_EXPERTS - 1)
    j = (blk_start - pstarts[blk_e])[:, None] + jnp.arange(MOE_BLOCK, dtype=jnp.int32)[None, :]
    valid = (j >= 0) & (j < counts[blk_e][:, None])
    src = jnp.clip(starts[blk_e][:, None] + j, 0, n_assign - 1)
    slot_tok = jnp.where(valid, order[src] // EXPERT_TOP_K, 0).astype(jnp.int32).reshape(-1)
    return slots.astype(jnp.int32), slot_tok, blk_e.astype(jnp.int32)


def _moe_and_final(x2, h, rinfo, w1, w3, w2, ln_final_g):
    slots, slot_tok, blk_e = _moe_plan(rinfo)
    ys = _moe_experts(h, slot_tok, blk_e, w1, w3, w2)
    return _moe_combine(x2, rinfo, slots, ys, ln_final_g)


def _rope_tables(pos):
    half = HEAD_DIM // 2
    inv = 1.0 / (ROPE_THETA ** (jnp.arange(half, dtype=F32) / half))
    ang = pos.astype(F32)[:, None] * inv[None, :]
    cos = jnp.cos(ang)
    sin = jnp.sin(ang)
    cos_t = jnp.tile(jnp.concatenate([cos, cos], axis=1), (1, LANES // HEAD_DIM))
    sin_t = jnp.tile(jnp.concatenate([-sin, sin], axis=1), (1, LANES // HEAD_DIM))
    return cos_t, sin_t


def _lambda_init(layer_idx):
    return 0.8 - 0.6 * math.exp(-0.3 * layer_idx)


def _split_hi_lo(w):
    hi = w.astype(BF16)
    return hi, (w - hi.astype(F32)).astype(BF16)


def kernel(x_prompt, x_sample, cache_k, cache_v, state_conv, cache_mem_k, cache_mem_v, page_table, mem_prompt, ln_mix_g, w_in, lambda_q1, lambda_k1, lambda_q2, lambda_k2, subln_g, conv_w, w_proj_attn, w_proj_conv, w_out, ln_mem_g, ln_memkv_g, w_mq, w_mk, w_mv, w_mo, ln_ffn_g, w_group, b_group, w_router, b_router, w1, w3, w2, ln_final_g):
    b, s, d = x_prompt.shape
    db, dsq, _ = x_sample.shape
    depth = w_in.shape[0]
    n_pool, page = cache_k.shape[1], cache_k.shape[2]
    past = page_table.shape[1] * page
    assert d == 2 * N_HEADS * HEAD_DIM and w_in.shape[2] == 8 * d
    assert depth == 1, "only the last layer feeds the final norm; deeper stacks are not supported"
    assert CONV_WIDTH - 1 <= dsq <= SAMPLE_ROWS <= page
    pad = SAMPLE_ROWS - dsq

    cos_p, sin_p = _rope_tables(jnp.arange(s, dtype=jnp.int32))
    pos_s = past + jnp.minimum(jnp.arange(SAMPLE_ROWS, dtype=jnp.int32), dsq - 1)
    cos_s, sin_s = (jnp.tile(a, (db, 1)) for a in _rope_tables(pos_s))

    xp = x_prompt.reshape(b * s, d)
    xs = jnp.pad(x_sample, ((0, 0), (0, pad), (0, 0))).reshape(db * SAMPLE_ROWS, d)
    outs_p, outs_s = [], []
    for l in range(depth):
        lam_init = _lambda_init(l)
        lamv = jnp.pad(jnp.stack([lambda_q1[l], lambda_k1[l], lambda_q2[l], lambda_k2[l]]),
                       ((0, SUBLANES - 4), (0, LANES - HEAD_DIM)))
        w_in_b = w_in[l].astype(BF16)
        wr = jnp.pad(jnp.concatenate([w_router[l], w_group[l]], axis=1), ((0, 0), (0, LANES - N_EXPERTS - N_GROUPS)))
        br = jnp.pad(jnp.concatenate([b_router[l], b_group[l]]), (0, LANES - N_EXPERTS - N_GROUPS)).reshape(1, LANES)
        wr_hi, wr_lo = _split_hi_lo(wr)
        wts = (w_proj_attn[l].astype(BF16), w_proj_conv[l].astype(BF16), w_out[l].astype(BF16),
               ln_mem_g[l].reshape(1, d), w_mq[l].astype(BF16), w_mo[l].astype(BF16),
               ln_ffn_g[l].reshape(1, d), wr_hi, wr_lo, br)

        q, kf, kb, vf, vb, bc, sa, sc, tail = _mixer_in(xp, ln_mix_g[l], w_in_b, cos_p, sin_p, conv_w[l], seq_len=s)
        o = _prompt_attention(q.reshape(b, s, d), kb.reshape(b, s, d), vb.reshape(b, s, d), lamv, subln_g[l], lam_init)
        mkf, mvf, mkb, mvb = _memory_kv(mem_prompt, ln_memkv_g[l], w_mk[l].astype(BF16), w_mv[l].astype(BF16))
        x2, h, rinfo = _prompt_merge(xp, o.reshape(b * s, d), bc, sa, sc, mkb, mvb, wts, seq_len=s)
        nk_p = kf.reshape(b, 2 * N_HEADS, HEAD_DIM, s).transpose(0, 3, 1, 2)
        nv_p = vf.reshape(b, s, N_HEADS, V_DIM)
        nc_p = tail[:, SUBLANES - (CONV_WIDTH - 1):, :]
        nmk_p = mkf.reshape(b, -1, MEM_HEADS, d // MEM_HEADS)
        nmv_p = mvf.reshape(b, -1, MEM_HEADS, d // MEM_HEADS)
        xp_moe = (x2, h, rinfo)

        st = state_conv[l]
        zrow = jnp.zeros((db, SAMPLE_ROWS - 1, d), F32)
        s1 = jnp.concatenate([st[:, 1:2], zrow], axis=1).reshape(db * SAMPLE_ROWS, d)
        s2 = jnp.concatenate([st, zrow[:, 1:]], axis=1).reshape(db * SAMPLE_ROWS, d)
        q, kf, kb, vf, vb, bc, sa, sc, u = _mixer_in(xs, ln_mix_g[l], w_in_b, cos_s, sin_s, conv_w[l],
                                                      seq_len=SAMPLE_ROWS, state_rows=(s1, s2))
        q4 = q.reshape(db, SAMPLE_ROWS, 2 * N_HEADS, HEAD_DIM)
        eye = jnp.eye(2 * N_HEADS, dtype=BF16)
        q2 = jnp.einsum('brhd,hg->bhrgd', q4, eye).reshape(db, 2 * N_HEADS * SAMPLE_ROWS, d)
        new_page = lambda a: jnp.pad(a.reshape(db, SAMPLE_ROWS, d), ((0, 0), (0, page - SAMPLE_ROWS), (0, 0)))
        o = _decode_attention(page_table, q2, new_page(kf).transpose(0, 2, 1), new_page(vf), cache_k, cache_v, l,
                              lamv, subln_g[l], lam_init, dsq)
        x2, h, rinfo = _sample_merge(xs, o.reshape(db * SAMPLE_ROWS, d), bc, sa, sc,
                                     cache_mem_k[l].reshape(db, -1, d), cache_mem_v[l].reshape(db, -1, d),
                                     wts, n_seq=db)
        real = lambda a: a.reshape(db, SAMPLE_ROWS, -1)[:, :dsq].reshape(db * dsq, -1)
        nk_s = real(kf).reshape(db, dsq, 2 * N_HEADS, HEAD_DIM)
        nv_s = real(vf).reshape(db, dsq, N_HEADS, V_DIM)
        nc_s = u.reshape(db, SAMPLE_ROWS, d)[:, dsq - (CONV_WIDTH - 1):dsq]
        xs_moe = (real(x2), real(h).reshape(db * dsq, d // LANES, LANES), real(rinfo))

        outs_p.append((nk_p, nv_p, nc_p, nmk_p, nmv_p))
        outs_s.append((nk_s, nv_s, nc_s))

    y_prompt = _moe_and_final(*xp_moe, w1[depth - 1], w3[depth - 1], w2[depth - 1], ln_final_g).reshape(b, s, d)
    y_sample = _moe_and_final(*xs_moe, w1[depth - 1], w3[depth - 1], w2[depth - 1], ln_final_g).reshape(db, dsq, d)
    stack = lambda items, j: jnp.stack([it[j] for it in items])
    return (y_prompt, y_sample,
            stack(outs_p, 0), stack(outs_p, 1), stack(outs_p, 2), stack(outs_p, 3), stack(outs_p, 4),
            stack(outs_s, 0), stack(outs_s, 1), stack(outs_s, 2))
```

```python
import functools
import math

import jax
import jax.numpy as jnp
from jax import lax
from jax.experimental import pallas as pl
from jax.experimental.pallas import tpu as pltpu

N_HEADS = 8
HEAD_DIM = 64
V_DIM = 2 * HEAD_DIM
ROPE_THETA = 10000.0
CONV_WIDTH = 3
MEM_HEADS = 4
N_GROUPS = 4
EXPERTS_PER_GROUP = 8
N_EXPERTS = N_GROUPS * EXPERTS_PER_GROUP
EXPERT_TOP_K = 2
MOE_BLOCK = 128
EPS = 1e-6
SUBLN_EPS = 1e-5

LANES = 128
SUBLANES = 8
VMEM_LIMIT_BYTES = 56 * 1024 * 1024

MIX_ROWS = 512
ATTN_TILE = 1024
PAGES_PER_STEP = 8
SAMPLE_ROWS = 8
COMBINE_ROWS = 128
MOE_BUFFERS = 3

NEG = -1e30
Q_SCALE = HEAD_DIM ** -0.5 * math.log2(math.e)
F32 = jnp.float32
BF16 = jnp.bfloat16


def _params(sem):
    return pltpu.CompilerParams(dimension_semantics=sem, vmem_limit_bytes=VMEM_LIMIT_BYTES)


def _rms(x, g, eps):
    return x * lax.rsqrt(jnp.mean(x * x, axis=-1, keepdims=True) + eps) * g


def _dot(a, b):
    return jnp.dot(a, b, preferred_element_type=F32)


def _dot_nt(a, b):
    return lax.dot_general(a, b, (((1,), (1,)), ((), ())), preferred_element_type=F32)


def _const_spec(shape):
    nd = len(shape)
    return pl.BlockSpec(shape, lambda *_: (0,) * nd, pipeline_mode=pl.Buffered(1))


def _lambda_value(lamv, lam_init):
    a = jnp.sum(lamv[0:1, :] * lamv[1:2, :], axis=1, keepdims=True)
    b = jnp.sum(lamv[2:3, :] * lamv[3:4, :], axis=1, keepdims=True)
    return jnp.exp(a) - jnp.exp(b) + lam_init


def _head_norm(o, g, lam_init):
    return _rms(o, g, SUBLN_EPS) * (1.0 - lam_init)


def _mixer_in_kernel(*refs, tm, d, seq_tiles, period):
    if period:
        (x_ref, g_ref, w_ref, cos_ref, sin_ref, cw_ref, s1_ref, s2_ref,
         q_ref, kf_ref, kb_ref, vf_ref, vb_ref, bc_ref, sa_ref, sc_ref, tail_ref, ubuf_ref) = refs
    else:
        (x_ref, g_ref, w_ref, cos_ref, sin_ref, cw_ref,
         q_ref, kf_ref, kb_ref, vf_ref, vb_ref, bc_ref, sa_ref, sc_ref, tail_ref, ubuf_ref) = refs
    i = pl.program_id(0)
    h = _rms(x_ref[...], g_ref[...], EPS).astype(BF16)

    def proj(j):
        return _dot(h, w_ref[:, j * d:(j + 1) * d])

    cos = cos_ref[...]
    sin = sin_ref[...]
    lane = lax.broadcasted_iota(jnp.int32, (tm, LANES), 1)
    first_half = (lane % HEAD_DIM) < (HEAD_DIM // 2)

    def rope_chunks(z):
        for c in range(d // LANES):
            zc = z[:, c * LANES:(c + 1) * LANES]
            zr = jnp.where(first_half, pltpu.roll(zc, LANES - HEAD_DIM // 2, 1), pltpu.roll(zc, HEAD_DIM // 2, 1))
            yield c, zc * cos + zr * sin

    zq = proj(0)
    for c, y in rope_chunks(zq):
        q_ref[:, c * LANES:(c + 1) * LANES] = (y * Q_SCALE).astype(BF16)
    zk = proj(1)
    for c, y in rope_chunks(zk):
        if seq_tiles:
            kf_ref[0, c * LANES:(c + 1) * LANES, :] = y.T
        else:
            kf_ref[:, c * LANES:(c + 1) * LANES] = y
        kb_ref[:, c * LANES:(c + 1) * LANES] = y.astype(BF16)
    zv = proj(2)
    vf_ref[...] = zv
    vb_ref[...] = zv.astype(BF16)

    b_gate = proj(3)
    u = proj(4) * proj(5)
    @pl.when(i % max(seq_tiles, 1) == 0)
    def _():
        ubuf_ref[0:SUBLANES, :] = jnp.zeros((SUBLANES, d), F32)
    ubuf_ref[SUBLANES:SUBLANES + tm, :] = u
    p1 = ubuf_ref[SUBLANES - 1:SUBLANES - 1 + tm, :]
    p2 = ubuf_ref[SUBLANES - 2:SUBLANES - 2 + tm, :]
    if period:
        r = lax.broadcasted_iota(jnp.int32, (tm, 1), 0) % period
        p1 = jnp.where(r == 0, s1_ref[...], p1)
        p2 = jnp.where(r < 2, s2_ref[...], p2)
    cw = cw_ref[...]
    y_conv = p2 * cw[0:1, :] + p1 * cw[1:2, :] + u * cw[2:3, :]
    bc_ref[...] = (b_gate * y_conv).astype(BF16)
    if seq_tiles:
        last = ubuf_ref[tm:tm + SUBLANES, :]
        ubuf_ref[0:SUBLANES, :] = last
        tail_ref[0] = last
    else:
        tail_ref[...] = u

    sa_ref[...] = jax.nn.sigmoid(proj(6)).astype(BF16)
    sc_ref[...] = jax.nn.sigmoid(proj(7)).astype(BF16)


def _mixer_in(x2d, ln_g, w_in_b, cos_t, sin_t, conv_w, *, seq_len, state_rows=None):
    t, d = x2d.shape
    tm = min(MIX_ROWS, t)
    assert t % tm == 0
    n_tiles = t // tm
    if state_rows is None:
        assert seq_len % tm == 0
        seq_tiles, period = seq_len // tm, 0
        n_seq = t // seq_len
        pos_tiles = seq_tiles
    else:
        assert tm % seq_len == 0 and n_tiles == 1
        seq_tiles, period = 0, seq_len
        pos_tiles = 1
    row = lambda i: (i, 0)
    pos = lambda i: (i % pos_tiles, 0)
    in_specs = [
        pl.BlockSpec((tm, d), row),
        _const_spec((1, d)),
        _const_spec(w_in_b.shape),
        pl.BlockSpec((tm, LANES), pos),
        pl.BlockSpec((tm, LANES), pos),
        _const_spec((SUBLANES, d)),
    ]
    args = [x2d, ln_g.reshape(1, d), w_in_b, cos_t, sin_t, jnp.pad(conv_w, ((0, SUBLANES - CONV_WIDTH), (0, 0)))]
    bspec = pl.BlockSpec((tm, d), row)
    sd = lambda dt: jax.ShapeDtypeStruct((t, d), dt)
    if period:
        in_specs += [pl.BlockSpec((tm, d), row), pl.BlockSpec((tm, d), row)]
        args += list(state_rows)
        tail_shape = jax.ShapeDtypeStruct((t, d), F32)
        tail_spec = pl.BlockSpec((tm, d), row)
        kf_shape, kf_spec = sd(F32), bspec
    else:
        tail_shape = jax.ShapeDtypeStruct((n_seq, SUBLANES, d), F32)
        tail_spec = pl.BlockSpec((1, SUBLANES, d), lambda i: (i // seq_tiles, 0, 0))
        kf_shape = jax.ShapeDtypeStruct((n_seq, d, seq_len), F32)
        kf_spec = pl.BlockSpec((1, d, tm), lambda i: (i // seq_tiles, 0, i % seq_tiles))
    return pl.pallas_call(
        functools.partial(_mixer_in_kernel, tm=tm, d=d, seq_tiles=seq_tiles, period=period),
        grid=(n_tiles,),
        in_specs=in_specs,
        out_specs=[bspec, kf_spec] + [bspec] * 6 + [tail_spec],
        out_shape=[sd(BF16), kf_shape, sd(BF16), sd(F32), sd(BF16), sd(BF16), sd(BF16), sd(BF16), tail_shape],
        scratch_shapes=[pltpu.VMEM((tm + 2 * SUBLANES, d), F32)],
        compiler_params=_params(("arbitrary",)),
        name="mixer_in",
    )(*args)


def _prompt_attn_kernel(lamv_ref, g_ref, q_ref, k_ref, v_ref, o_ref,
                        s0_ref, s1_ref, mx0_ref, mx1_ref, m_ref, acc_ref, *, tile, lam_init):
    qi = pl.program_id(2)
    q = q_ref[0]
    lane = lax.broadcasted_iota(jnp.int32, q.shape, 1)
    zero = jnp.zeros_like(q)
    qs = (jnp.where(lane < HEAD_DIM, q, zero), jnp.where(lane >= HEAD_DIM, q, zero))
    m_ref[...] = jnp.full(m_ref.shape, NEG, F32)
    acc_ref[...] = jnp.zeros(acc_ref.shape, F32)
    n_chunks = tile // LANES

    s_bufs = (s0_ref, s1_ref)
    mx_bufs = (mx0_ref, mx1_ref)

    def scores(ki, buf, diagonal=False):
        off = pl.multiple_of(ki * tile, tile)
        k = k_ref[0, pl.ds(off, tile), :]
        for j in range(2):
            s = _dot_nt(qs[j], k)
            if diagonal:
                row = lax.broadcasted_iota(jnp.int32, (tile, tile), 0)
                col = lax.broadcasted_iota(jnp.int32, (tile, tile), 1)
                s = jnp.where(col <= row, s, NEG)
            s_bufs[buf][j] = s
            mx_bufs[buf][j] = functools.reduce(
                jnp.maximum, [s[:, c * LANES:(c + 1) * LANES] for c in range(n_chunks)])

    def absorb(ki, buf):
        off = pl.multiple_of(ki * tile, tile)
        v = v_ref[0, pl.ds(off, tile), :]
        v_ones = jnp.concatenate([v, jnp.ones_like(v)], axis=1)
        for j in range(2):
            m_old = m_ref[j]
            m_new = jnp.maximum(m_old, jnp.max(mx_bufs[buf][j], axis=1, keepdims=True))
            alpha = jnp.exp2(m_old - m_new)
            ps = [jnp.exp2(s_bufs[buf][j, :, c * LANES:(c + 1) * LANES] - m_new) for c in range(n_chunks)]
            p = jnp.concatenate(ps, axis=1).astype(BF16)
            acc_ref[j] = jnp.concatenate([alpha, alpha], axis=1) * acc_ref[j] + _dot(p, v_ones)
            m_ref[j] = m_new

    odd = qi % 2 == 1

    @pl.when(qi == 0)
    def _():
        scores(0, 0, diagonal=True)
        absorb(0, 0)

    @pl.when(qi > 0)
    def _():
        scores(0, 0)

    def pair(t, carry):
        i = 2 * t + 1
        scores(i, 1)
        absorb(i - 1, 0)
        scores(i + 1, 0)
        absorb(i, 1)
        return carry

    lax.fori_loop(0, (qi - 1) // 2, pair, 0)

    @pl.when(odd)
    def _():
        scores(qi, 1, diagonal=True)
        absorb(qi - 1, 0)
        absorb(qi, 1)

    @pl.when((qi > 0) & jnp.logical_not(odd))
    def _():
        scores(qi - 1, 1)
        absorb(qi - 2, 0)
        scores(qi, 0, diagonal=True)
        absorb(qi - 1, 1)
        absorb(qi, 0)

    lam = _lambda_value(lamv_ref[...], lam_init)
    o = acc_ref[0, :, :V_DIM] / acc_ref[0, :, V_DIM:] - lam * (acc_ref[1, :, :V_DIM] / acc_ref[1, :, V_DIM:])
    o_ref[0] = _head_norm(o, g_ref[...], lam_init).astype(BF16)


def _prompt_attention(q, k, v, lamv, subln_g, lam_init):
    b, s, d = q.shape
    tile = min(ATTN_TILE, s)
    assert s % tile == 0
    nq = s // tile
    return pl.pallas_call(
        functools.partial(_prompt_attn_kernel, tile=tile, lam_init=lam_init),
        grid=(b, N_HEADS, nq),
        in_specs=[
            _const_spec((SUBLANES, LANES)),
            _const_spec((1, V_DIM)),
            pl.BlockSpec((1, tile, LANES), lambda bi, h, qi: (bi, qi, h)),
            pl.BlockSpec((1, s, LANES), lambda bi, h, qi: (bi, 0, h)),
            pl.BlockSpec((1, s, LANES), lambda bi, h, qi: (bi, 0, h)),
        ],
        out_specs=pl.BlockSpec((1, tile, LANES), lambda bi, h, qi: (bi, qi, h)),
        out_shape=jax.ShapeDtypeStruct((b, s, d), BF16),
        scratch_shapes=[
            pltpu.VMEM((2, tile, tile), F32),
            pltpu.VMEM((2, tile, tile), F32),
            pltpu.VMEM((2, tile, LANES), F32),
            pltpu.VMEM((2, tile, LANES), F32),
            pltpu.VMEM((2, tile, LANES), F32),
            pltpu.VMEM((2, tile, 2 * V_DIM), F32),
        ],
        compiler_params=_params(("arbitrary", "arbitrary", "arbitrary")),
        name="prompt_attention",
    )(lamv, subln_g.reshape(1, V_DIM), q, k, v)


def _decode_attn_kernel(pt_ref, lamv_ref, g_ref, q_ref, kn_ref, vn_ref, *refs, n_pages_step, n_new, lam_init):
    k_refs = refs[:n_pages_step]
    v_refs = refs[n_pages_step:2 * n_pages_step]
    o_ref, m_ref, l_ref, acc_ref = refs[2 * n_pages_step:]
    p_id = pl.program_id(1)
    qbd = q_ref[0]
    n_rows = qbd.shape[0]
    rows_per_vh = 2 * SAMPLE_ROWS

    def update(kt, v, visible=None):
        n_chunks = kt.shape[1] // LANES
        s = _dot(qbd, kt.astype(BF16))
        if visible is not None:
            s = jnp.where(visible, s, NEG)
        chunks = [s[:, c * LANES:(c + 1) * LANES] for c in range(n_chunks)]
        m_old = m_ref[...]
        m_new = jnp.maximum(m_old, jnp.max(functools.reduce(jnp.maximum, chunks), axis=1, keepdims=True))
        alpha = jnp.exp2(m_old - m_new)
        ps = [jnp.exp2(c - m_new) for c in chunks]
        l_ref[...] = alpha * l_ref[...] + functools.reduce(lambda a, b: a + b, ps)
        m_ref[...] = m_new
        p = jnp.concatenate(ps, axis=1).astype(BF16)
        vb = v.astype(BF16)
        for vh in range(N_HEADS):
            rows = slice(vh * rows_per_vh, (vh + 1) * rows_per_vh)
            pv = _dot(p[rows, :], vb[:, vh * V_DIM:(vh + 1) * V_DIM])
            acc_ref[rows, :] = alpha[rows, :] * acc_ref[rows, :] + pv

    @pl.when(p_id == 0)
    def _():
        m_ref[...] = jnp.full(m_ref.shape, NEG, F32)
        l_ref[...] = jnp.zeros(l_ref.shape, F32)
        acc_ref[...] = jnp.zeros(acc_ref.shape, F32)
        shape = (n_rows, kn_ref.shape[2])
        r = lax.broadcasted_iota(jnp.int32, shape, 0) % SAMPLE_ROWS
        j = lax.broadcasted_iota(jnp.int32, shape, 1)
        update(kn_ref[0], vn_ref[0], (j <= r) & (j < n_new))

    d = qbd.shape[1]
    update(jnp.concatenate([r[...].reshape(d, r.shape[2]) for r in k_refs], axis=1),
           jnp.concatenate([pltpu.einshape("phd->p(hd)", r[...]) for r in v_refs], axis=0))

    @pl.when(p_id == pl.num_programs(1) - 1)
    def _():
        lam = _lambda_value(lamv_ref[...], lam_init)
        linv = 1.0 / jnp.sum(l_ref[...], axis=1, keepdims=True)
        g = g_ref[...]
        for vh in range(N_HEADS):
            r1 = slice((2 * vh) * SAMPLE_ROWS, (2 * vh + 1) * SAMPLE_ROWS)
            r2 = slice((2 * vh + 1) * SAMPLE_ROWS, (2 * vh + 2) * SAMPLE_ROWS)
            o1 = acc_ref[r1, :] * linv[r1, :]
            o2 = acc_ref[r2, :] * linv[r2, :]
            o_ref[0, :, vh * V_DIM:(vh + 1) * V_DIM] = _head_norm(o1 - lam * o2, g, lam_init)


def _decode_attention(page_table, q2, k_new, v_new, cache_k, cache_v, layer, lamv, subln_g, lam_init, n_new):
    db, n_rows, d = q2.shape
    n_pages = page_table.shape[1]
    nps = math.gcd(PAGES_PER_STEP, n_pages)
    steps = n_pages // nps
    cache_k = jnp.transpose(cache_k, (0, 1, 3, 4, 2))

    def page_spec(cache, j):
        return pl.BlockSpec((None, None) + cache.shape[2:], lambda b, p, pt: (layer, pt[b, p * nps + j], 0, 0, 0))

    per_seq = lambda a: pl.BlockSpec((1,) + a.shape[1:], lambda b, p, pt: (b,) + (0,) * (a.ndim - 1))
    grid_spec = pltpu.PrefetchScalarGridSpec(
        num_scalar_prefetch=1,
        grid=(db, steps),
        in_specs=[
            pl.BlockSpec((SUBLANES, LANES), lambda b, p, pt: (0, 0)),
            pl.BlockSpec((1, V_DIM), lambda b, p, pt: (0, 0)),
            per_seq(q2), per_seq(k_new), per_seq(v_new),
        ] + [page_spec(cache_k, j) for j in range(nps)] + [page_spec(cache_v, j) for j in range(nps)],
        out_specs=pl.BlockSpec((1, SAMPLE_ROWS, d), lambda b, p, pt: (b, 0, 0)),
        scratch_shapes=[
            pltpu.VMEM((n_rows, LANES), F32),
            pltpu.VMEM((n_rows, LANES), F32),
            pltpu.VMEM((n_rows, V_DIM), F32),
        ],
    )
    return pl.pallas_call(
        functools.partial(_decode_attn_kernel, n_pages_step=nps, n_new=n_new, lam_init=lam_init),
        grid_spec=grid_spec,
        out_shape=jax.ShapeDtypeStruct((db, SAMPLE_ROWS, d), F32),
        compiler_params=_params(("arbitrary", "arbitrary")),
        name="decode_attention",
    )(page_table, lamv, subln_g.reshape(1, V_DIM), q2, k_new, v_new,
      *([cache_k] * nps), *([cache_v] * nps))


def _memory_kv_kernel(mem_ref, g_ref, wk_ref, wv_ref, kf_ref, vf_ref, kb_ref, vb_ref):
    h = _rms(mem_ref[0], g_ref[...], EPS).astype(BF16)
    k = _dot(h, wk_ref[...])
    v = _dot(h, wv_ref[...])
    kf_ref[0] = k
    vf_ref[0] = v
    kb_ref[0] = k.astype(BF16)
    vb_ref[0] = v.astype(BF16)


def _memory_kv(mem, ln_g, w_mk_b, w_mv_b):
    b, m, d = mem.shape
    spec = pl.BlockSpec((1, m, d), lambda i: (i, 0, 0))
    return pl.pallas_call(
        _memory_kv_kernel,
        grid=(b,),
        in_specs=[spec, _const_spec((1, d)), _const_spec((d, d)), _const_spec((d, d))],
        out_specs=[spec] * 4,
        out_shape=[jax.ShapeDtypeStruct((b, m, d), F32)] * 2 + [jax.ShapeDtypeStruct((b, m, d), BF16)] * 2,
        compiler_params=_params(("arbitrary",)),
        name="memory_kv",
    )(mem, ln_g.reshape(1, d), w_mk_b, w_mv_b)


def _merge_vals(x, o, bc, sa, sc, wpa, wpc, wout):
    mixed = sa.astype(F32) * _dot(o.astype(BF16), wpa) + sc.astype(F32) * _dot(bc, wpc)
    return x + _dot(mixed.astype(BF16), wout)


def _memory_attn_vals(qm, mk, mv):
    d = qm.shape[1]
    hd = d // MEM_HEADS
    outs = []
    for h in range(MEM_HEADS):
        sl = slice(h * hd, (h + 1) * hd)
        s = _dot_nt(qm[:, sl].astype(BF16), mk[:, sl]) * (hd ** -0.5)
        e = jnp.exp(s - jnp.max(s, axis=1, keepdims=True))
        p = e / jnp.sum(e, axis=1, keepdims=True)
        outs.append(_dot(p.astype(BF16), mv[:, sl]))
    return jnp.concatenate(outs, axis=1)


def _router_vals(h, wr_hi, wr_lo, br):
    h_hi = h.astype(BF16)
    h_lo = (h - h_hi.astype(F32)).astype(BF16)
    lg = _dot(h_hi, wr_hi) + (_dot(h_lo, wr_hi) + _dot(h_hi, wr_lo)) + br
    lane = lax.broadcasted_iota(jnp.int32, lg.shape, 1)
    big = jnp.int32(4 * LANES)
    is_g = (lane >= N_EXPERTS) & (lane < N_EXPERTS + N_GROUPS)
    gl = jnp.where(is_g, lg, NEG)
    gmax = jnp.max(gl, axis=1, keepdims=True)
    gidx = jnp.min(jnp.where(is_g & (gl == gmax), lane - N_EXPERTS, big), axis=1, keepdims=True)
    gprob = 1.0 / jnp.sum(jnp.where(is_g, jnp.exp(gl - gmax), 0.0), axis=1, keepdims=True)
    in_group = (lane < N_EXPERTS) & ((lane // EXPERTS_PER_GROUP) == gidx)
    el = jnp.where(in_group, lg, NEG)
    l1 = jnp.max(el, axis=1, keepdims=True)
    i1 = jnp.min(jnp.where(in_group & (el == l1), lane, big), axis=1, keepdims=True)
    rest = in_group & (lane != i1)
    el2 = jnp.where(rest, lg, NEG)
    l2 = jnp.max(el2, axis=1, keepdims=True)
    i2 = jnp.min(jnp.where(rest & (el2 == l2), lane, big), axis=1, keepdims=True)
    e21 = jnp.exp(l2 - l1)
    p1 = 1.0 / (1.0 + e21)
    g1 = gprob * p1
    g2 = gprob * (e21 * p1)
    return jnp.where(lane == 0, i1.astype(F32),
                     jnp.where(lane == 1, i2.astype(F32),
                               jnp.where(lane == 2, g1, jnp.where(lane == 3, g2, 0.0))))


def _store_token_tiles(ref, x):
    for c in range(ref.shape[1]):
        ref[:, c, :] = x[:, c * LANES:(c + 1) * LANES]


def _post_vals(x1, om, wmo, gffn, wr_hi, wr_lo, br):
    x2 = x1 + _dot(om.astype(BF16), wmo)
    h = _rms(x2, gffn, EPS)
    return x2, h, _router_vals(h, wr_hi, wr_lo, br)


def _prompt_merge_kernel(x_ref, o_ref, bc_ref, sa_ref, sc_ref, mk_ref, mv_ref,
                         wpa_ref, wpc_ref, wout_ref, gmem_ref, wmq_ref, wmo_ref,
                         gffn_ref, wrh_ref, wrl_ref, br_ref,
                         x2_ref, h_ref, ri_ref):
    x1 = _merge_vals(x_ref[...], o_ref[...], bc_ref[...], sa_ref[...], sc_ref[...],
                     wpa_ref[...], wpc_ref[...], wout_ref[...])
    qm = _dot(_rms(x1, gmem_ref[...], EPS).astype(BF16), wmq_ref[...])
    om = _memory_attn_vals(qm, mk_ref[0], mv_ref[0])
    x2, h, ri = _post_vals(x1, om, wmo_ref[...], gffn_ref[...], wrh_ref[...], wrl_ref[...], br_ref[...])
    x2_ref[...] = x2
    _store_token_tiles(h_ref, h)
    ri_ref[...] = ri


def _prompt_merge(x2d, o, bc, sa, sc, mk_b, mv_b, wts, *, seq_len):
    t, d = x2d.shape
    tm = min(MIX_ROWS, seq_len)
    assert seq_len % tm == 0
    seq_tiles = seq_len // tm
    n_mem = mk_b.shape[1]
    row = pl.BlockSpec((tm, d), lambda i: (i, 0))
    mem = pl.BlockSpec((1, n_mem, d), lambda i: (i // seq_tiles, 0, 0))
    return pl.pallas_call(
        _prompt_merge_kernel,
        grid=(t // tm,),
        in_specs=[row] * 5 + [mem] * 2 + [_const_spec(w.shape) for w in wts],
        out_specs=[row, pl.BlockSpec((tm, d // LANES, LANES), lambda i: (i, 0, 0)),
                   pl.BlockSpec((tm, LANES), lambda i: (i, 0))],
        out_shape=[jax.ShapeDtypeStruct((t, d), F32), jax.ShapeDtypeStruct((t, d // LANES, LANES), F32),
                   jax.ShapeDtypeStruct((t, LANES), F32)],
        compiler_params=_params(("arbitrary",)),
        name="prompt_merge",
    )(x2d, o, bc, sa, sc, mk_b, mv_b, *wts)


def _sample_merge_kernel(x_ref, o_ref, bc_ref, sa_ref, sc_ref, wpa_ref, wpc_ref, wout_ref, gmem_ref, wmq_ref,
                         x1_ref, qm_ref):
    x1 = _merge_vals(x_ref[...], o_ref[...], bc_ref[...], sa_ref[...], sc_ref[...],
                     wpa_ref[...], wpc_ref[...], wout_ref[...])
    x1_ref[...] = x1
    qm_ref[...] = _dot(_rms(x1, gmem_ref[...], EPS).astype(BF16), wmq_ref[...])


def _sample_memattn_kernel(qm_ref, mk_ref, mv_ref, om_ref):
    om_ref[0] = _memory_attn_vals(qm_ref[0], mk_ref[0].astype(BF16), mv_ref[0].astype(BF16))


def _sample_post_kernel(x1_ref, om_ref, wmo_ref, gffn_ref, wrh_ref, wrl_ref, br_ref, x2_ref, h_ref, ri_ref):
    x2, h, ri = _post_vals(x1_ref[...], om_ref[...], wmo_ref[...], gffn_ref[...],
                           wrh_ref[...], wrl_ref[...], br_ref[...])
    x2_ref[...] = x2
    _store_token_tiles(h_ref, h)
    ri_ref[...] = ri


def _sample_merge(x2d, o, bc, sa, sc, mem_k, mem_v, wts, *, n_seq):
    t, d = x2d.shape
    wpa, wpc, wout, gmem, wmq, wmo, gffn, wrh, wrl, br = wts
    full = pl.BlockSpec((t, d), lambda i: (0, 0))
    x1, qm = pl.pallas_call(
        _sample_merge_kernel,
        grid=(1,),
        in_specs=[full] * 5 + [_const_spec(w.shape) for w in (wpa, wpc, wout, gmem, wmq)],
        out_specs=[full, full],
        out_shape=[jax.ShapeDtypeStruct((t, d), F32)] * 2,
        compiler_params=_params(("arbitrary",)),
        name="sample_merge",
    )(x2d, o, bc, sa, sc, wpa, wpc, wout, gmem, wmq)
    rows = t // n_seq
    n_mem = mem_k.shape[1]
    seq = pl.BlockSpec((1, rows, d), lambda i: (i, 0, 0))
    mem = pl.BlockSpec((1, n_mem, d), lambda i: (i, 0, 0))
    om = pl.pallas_call(
        _sample_memattn_kernel,
        grid=(n_seq,),
        in_specs=[seq, mem, mem],
        out_specs=seq,
        out_shape=jax.ShapeDtypeStruct((n_seq, rows, d), F32),
        compiler_params=_params(("arbitrary",)),
        name="sample_memattn",
    )(qm.reshape(n_seq, rows, d), mem_k, mem_v)
    return pl.pallas_call(
        _sample_post_kernel,
        grid=(1,),
        in_specs=[full, full] + [_const_spec(w.shape) for w in (wmo, gffn, wrh, wrl, br)],
        out_specs=[full, pl.BlockSpec((t, d // LANES, LANES), lambda i: (0, 0, 0)),
                   pl.BlockSpec((t, LANES), lambda i: (0, 0))],
        out_shape=[jax.ShapeDtypeStruct((t, d), F32), jax.ShapeDtypeStruct((t, d // LANES, LANES), F32),
                   jax.ShapeDtypeStruct((t, LANES), F32)],
        compiler_params=_params(("arbitrary",)),
        name="sample_post",
    )(x1, om.reshape(t, d), wmo, gffn, wrh, wrl, br)


def _row_copy(src_hbm, idx, dst, r, sem):
    return pltpu.make_async_copy(src_hbm.at[pl.ds(idx, 1), :], dst.at[pl.ds(r, 1), :], sem)


def _token_copy(src_hbm, idx, dst, r, sem):
    return pltpu.make_async_copy(src_hbm.at[idx], dst.at[r], sem)


def _moe_kernel(blk_e_ref, tok0_ref, tok1_ref, tok_next_ref, h_hbm, w1_ref, w3_ref, w2_ref, ys_ref,
                xbuf, w1b_ref, w3b_ref, w2b_ref, sem):
    i = pl.program_id(0)
    last = pl.num_programs(0) - 1
    cur, nxt = i % MOE_BUFFERS, (i + MOE_BUFFERS - 1) % MOE_BUFFERS
    d, de = w1_ref.shape[1], w1_ref.shape[2]

    def wait_rows(buf):
        for r in range(MOE_BLOCK):
            _token_copy(h_hbm, 0, xbuf.at[buf], r, sem.at[buf]).wait()

    @pl.when(i == 0)
    def _():
        for b, toks in enumerate((tok0_ref, tok1_ref)):
            for r in range(MOE_BLOCK):
                _token_copy(h_hbm, toks[0, 0, r], xbuf.at[b], r, sem.at[b]).start()

    @pl.when((i == 0) | (blk_e_ref[i] != blk_e_ref[jnp.maximum(i - 1, 0)]))
    def _():
        w1b_ref[...] = w1_ref[0].astype(BF16)
        w3b_ref[...] = w3_ref[0].astype(BF16)
        w2b_ref[...] = w2_ref[0].astype(BF16)

    wait_rows(cur)
    x = pltpu.einshape("tcl->t(cl)", xbuf[cur]).astype(BF16)
    rows = iter(range(MOE_BLOCK))
    n_chunks = 4
    per_chunk = MOE_BLOCK // (3 * n_chunks)

    def start_rows(n):
        for _ in range(n):
            r = next(rows)
            _token_copy(h_hbm, tok_next_ref[0, 0, r], xbuf.at[nxt], r, sem.at[nxt]).start(priority=r % 2)

    def up(w_ref):
        acc = None
        for c in range(n_chunks):
            start_rows(per_chunk)
            ks = slice(c * (d // n_chunks), (c + 1) * (d // n_chunks))
            part = _dot(x[:, ks], w_ref[ks, :])
            acc = part if acc is None else acc + part
        return acc

    a = up(w1b_ref)
    b = up(w3b_ref)
    hmid = (jax.nn.silu(a) * b).astype(BF16)
    for c in range(n_chunks):
        start_rows(per_chunk)
        ns = slice(c * (d // n_chunks), (c + 1) * (d // n_chunks))
        ys_ref[:, ns] = _dot(hmid, w2b_ref[:, ns])
    start_rows(MOE_BLOCK - 3 * n_chunks * per_chunk)

    @pl.when(i == last)
    def _():
        for b in range(1, MOE_BUFFERS):
            wait_rows((i + b) % MOE_BUFFERS)


def _moe_experts(h, slot_tok, blk_e, w1, w3, w2):
    d = h.shape[1] * h.shape[2]
    n_blocks = blk_e.shape[0]
    de = w1.shape[2]
    toks = slot_tok.reshape(n_blocks, 1, MOE_BLOCK)
    tok_spec = lambda f: pl.BlockSpec((1, 1, MOE_BLOCK), f, memory_space=pltpu.SMEM)
    grid_spec = pltpu.PrefetchScalarGridSpec(
        num_scalar_prefetch=1,
        grid=(n_blocks,),
        in_specs=[
            tok_spec(lambda i, be: (i, 0, 0)),
            tok_spec(lambda i, be: (jnp.minimum(i + 1, n_blocks - 1), 0, 0)),
            tok_spec(lambda i, be: (jnp.minimum(i + MOE_BUFFERS - 1, n_blocks - 1), 0, 0)),
            pl.BlockSpec(memory_space=pl.ANY),
            pl.BlockSpec((1, d, de), lambda i, be: (be[i], 0, 0)),
            pl.BlockSpec((1, d, de), lambda i, be: (be[i], 0, 0)),
            pl.BlockSpec((1, de, d), lambda i, be: (be[i], 0, 0)),
        ],
        out_specs=pl.BlockSpec((MOE_BLOCK, d), lambda i, be: (i, 0)),
        scratch_shapes=[
            pltpu.VMEM((MOE_BUFFERS, MOE_BLOCK, d // LANES, LANES), F32),
            pltpu.VMEM((d, de), BF16), pltpu.VMEM((d, de), BF16), pltpu.VMEM((de, d), BF16),
            pltpu.SemaphoreType.DMA((MOE_BUFFERS,)),
        ],
    )
    return pl.pallas_call(
        _moe_kernel,
        grid_spec=grid_spec,
        out_shape=jax.ShapeDtypeStruct((n_blocks * MOE_BLOCK, d), F32),
        compiler_params=_params(("arbitrary",)),
        name="moe_experts",
    )(blk_e, toks, toks, toks, h, w1, w3, w2)


def _combine_kernel(slot_ref, slot_next_ref, x2_ref, ri_ref, g_ref, ys_hbm, y_ref, ybuf, sem, *, tm):
    i = pl.program_id(0)
    n = pl.num_programs(0)

    def gather(slots, buf):
        for r in range(2 * tm):
            _row_copy(ys_hbm, slots[0, 0, r], ybuf.at[buf], r, sem.at[buf]).start()

    @pl.when(i == 0)
    def _():
        gather(slot_ref, 0)

    @pl.when(i + 1 < n)
    def _():
        gather(slot_next_ref, (i + 1) % 2)

    buf = i % 2
    for r in range(2 * tm):
        _row_copy(ys_hbm, 0, ybuf.at[buf], r, sem.at[buf]).wait()
    ri = ri_ref[...]
    y = x2_ref[...] + ri[:, 2:3] * ybuf[buf, 0:tm, :] + ri[:, 3:4] * ybuf[buf, tm:2 * tm, :]
    y_ref[...] = _rms(y, g_ref[...], EPS)


def _moe_combine(x2, rinfo, slots, ys, ln_g):
    t, d = x2.shape
    tm = min(COMBINE_ROWS, t)
    assert t % tm == 0
    n = t // tm
    sl = slots.reshape(n, tm, 2).transpose(0, 2, 1).reshape(n, 1, 2 * tm)
    slot_spec = lambda f: pl.BlockSpec((1, 1, 2 * tm), f, memory_space=pltpu.SMEM)
    row = pl.BlockSpec((tm, d), lambda i: (i, 0))
    return pl.pallas_call(
        functools.partial(_combine_kernel, tm=tm),
        grid=(n,),
        in_specs=[
            slot_spec(lambda i: (i, 0, 0)),
            slot_spec(lambda i: (jnp.minimum(i + 1, n - 1), 0, 0)),
            row,
            pl.BlockSpec((tm, LANES), lambda i: (i, 0)),
            _const_spec((1, d)),
            pl.BlockSpec(memory_space=pl.ANY),
        ],
        out_specs=row,
        out_shape=jax.ShapeDtypeStruct((t, d), F32),
        scratch_shapes=[pltpu.VMEM((2, 2 * tm, d), F32), pltpu.SemaphoreType.DMA((2,))],
        compiler_params=_params(("arbitrary",)),
        name="moe_combine",
    )(sl, sl, x2, rinfo, ln_g.reshape(1, d), ys)


def _moe_plan(rinfo):
    t = rinfo.shape[0]
    eid = rinfo[:, :EXPERT_TOP_K].astype(jnp.int32).reshape(-1)
    n_assign = t * EXPERT_TOP_K
    experts = jnp.arange(N_EXPERTS, dtype=jnp.int32)
    order = jnp.argsort(eid, stable=True)
    pos = jnp.argsort(order)
    counts = jnp.sum((eid[:, None] == experts[None, :]).astype(jnp.int32), axis=0)
    starts = jnp.cumsum(counts) - counts
    padded = (counts + MOE_BLOCK - 1) // MOE_BLOCK * MOE_BLOCK
    pends = jnp.cumsum(padded)
    pstarts = pends - padded
    slots = (pos + (pstarts - starts)[eid]).reshape(t, EXPERT_TOP_K)
    n_blocks = -(-n_assign // MOE_BLOCK) + N_EXPERTS
    blk_start = jnp.arange(n_blocks, dtype=jnp.int32) * MOE_BLOCK
    blk_e = jnp.minimum(jnp.sum((pends[None, :] <= blk_start[:, None]).astype(jnp.int32), axis=1), N_EXPERTS - 1)
    j = (blk_start - pstarts[blk_e])[:, None] + jnp.arange(MOE_BLOCK, dtype=jnp.int32)[None, :]
    valid = (j >= 0) & (j < counts[blk_e][:, None])
    src = jnp.clip(starts[blk_e][:, None] + j, 0, n_assign - 1)
    filler = (blk_start[:, None] + jnp.arange(MOE_BLOCK, dtype=jnp.int32)[None, :]) % t
    slot_tok = jnp.where(valid, order[src] // EXPERT_TOP_K, filler).astype(jnp.int32).reshape(-1)
    return slots.astype(jnp.int32), slot_tok, blk_e.astype(jnp.int32)


def _moe_and_final(x2, h, rinfo, w1, w3, w2, ln_final_g):
    slots, slot_tok, blk_e = _moe_plan(rinfo)
    ys = _moe_experts(h, slot_tok, blk_e, w1, w3, w2)
    return _moe_combine(x2, rinfo, slots, ys, ln_final_g)


def _rope_tables(pos):
    half = HEAD_DIM // 2
    inv = 1.0 / (ROPE_THETA ** (jnp.arange(half, dtype=F32) / half))
    ang = pos.astype(F32)[:, None] * inv[None, :]
    cos = jnp.cos(ang)
    sin = jnp.sin(ang)
    cos_t = jnp.tile(jnp.concatenate([cos, cos], axis=1), (1, LANES // HEAD_DIM))
    sin_t = jnp.tile(jnp.concatenate([-sin, sin], axis=1), (1, LANES // HEAD_DIM))
    return cos_t, sin_t


def _lambda_init(layer_idx):
    return 0.8 - 0.6 * math.exp(-0.3 * layer_idx)


def _split_hi_lo(w):
    hi = w.astype(BF16)
    return hi, (w - hi.astype(F32)).astype(BF16)


def kernel(x_prompt, x_sample, cache_k, cache_v, state_conv, cache_mem_k, cache_mem_v, page_table, mem_prompt, ln_mix_g, w_in, lambda_q1, lambda_k1, lambda_q2, lambda_k2, subln_g, conv_w, w_proj_attn, w_proj_conv, w_out, ln_mem_g, ln_memkv_g, w_mq, w_mk, w_mv, w_mo, ln_ffn_g, w_group, b_group, w_router, b_router, w1, w3, w2, ln_final_g):
    b, s, d = x_prompt.shape
    db, dsq, _ = x_sample.shape
    depth = w_in.shape[0]
    n_pool, page = cache_k.shape[1], cache_k.shape[2]
    past = page_table.shape[1] * page
    assert d == 2 * N_HEADS * HEAD_DIM and w_in.shape[2] == 8 * d
    assert depth == 1, "only the last layer feeds the final norm; deeper stacks are not supported"
    assert CONV_WIDTH - 1 <= dsq <= SAMPLE_ROWS <= page
    pad = SAMPLE_ROWS - dsq

    cos_p, sin_p = _rope_tables(jnp.arange(s, dtype=jnp.int32))
    pos_s = past + jnp.minimum(jnp.arange(SAMPLE_ROWS, dtype=jnp.int32), dsq - 1)
    cos_s, sin_s = (jnp.tile(a, (db, 1)) for a in _rope_tables(pos_s))

    xp = x_prompt.reshape(b * s, d)
    xs = jnp.pad(x_sample, ((0, 0), (0, pad), (0, 0))).reshape(db * SAMPLE_ROWS, d)
    outs_p, outs_s = [], []
    for l in range(depth):
        lam_init = _lambda_init(l)
        lamv = jnp.pad(jnp.stack([lambda_q1[l], lambda_k1[l], lambda_q2[l], lambda_k2[l]]),
                       ((0, SUBLANES - 4), (0, LANES - HEAD_DIM)))
        w_in_b = w_in[l].astype(BF16)
        wr = jnp.pad(jnp.concatenate([w_router[l], w_group[l]], axis=1), ((0, 0), (0, LANES - N_EXPERTS - N_GROUPS)))
        br = jnp.pad(jnp.concatenate([b_router[l], b_group[l]]), (0, LANES - N_EXPERTS - N_GROUPS)).reshape(1, LANES)
        wr_hi, wr_lo = _split_hi_lo(wr)
        wts = (w_proj_attn[l].astype(BF16), w_proj_conv[l].astype(BF16), w_out[l].astype(BF16),
               ln_mem_g[l].reshape(1, d), w_mq[l].astype(BF16), w_mo[l].astype(BF16),
               ln_ffn_g[l].reshape(1, d), wr_hi, wr_lo, br)

        q, kf, kb, vf, vb, bc, sa, sc, tail = _mixer_in(xp, ln_mix_g[l], w_in_b, cos_p, sin_p, conv_w[l], seq_len=s)
        o = _prompt_attention(q.reshape(b, s, d), kb.reshape(b, s, d), vb.reshape(b, s, d), lamv, subln_g[l], lam_init)
        mkf, mvf, mkb, mvb = _memory_kv(mem_prompt, ln_memkv_g[l], w_mk[l].astype(BF16), w_mv[l].astype(BF16))
        x2, h, rinfo = _prompt_merge(xp, o.reshape(b * s, d), bc, sa, sc, mkb, mvb, wts, seq_len=s)
        nk_p = kf.reshape(b, 2 * N_HEADS, HEAD_DIM, s).transpose(0, 3, 1, 2)
        nv_p = vf.reshape(b, s, N_HEADS, V_DIM)
        nc_p = tail[:, SUBLANES - (CONV_WIDTH - 1):, :]
        nmk_p = mkf.reshape(b, -1, MEM_HEADS, d // MEM_HEADS)
        nmv_p = mvf.reshape(b, -1, MEM_HEADS, d // MEM_HEADS)
        xp_moe = (x2, h, rinfo)

        st = state_conv[l]
        zrow = jnp.zeros((db, SAMPLE_ROWS - 1, d), F32)
        s1 = jnp.concatenate([st[:, 1:2], zrow], axis=1).reshape(db * SAMPLE_ROWS, d)
        s2 = jnp.concatenate([st, zrow[:, 1:]], axis=1).reshape(db * SAMPLE_ROWS, d)
        q, kf, kb, vf, vb, bc, sa, sc, u = _mixer_in(xs, ln_mix_g[l], w_in_b, cos_s, sin_s, conv_w[l],
                                                      seq_len=SAMPLE_ROWS, state_rows=(s1, s2))
        q4 = q.reshape(db, SAMPLE_ROWS, 2 * N_HEADS, HEAD_DIM)
        eye = jnp.eye(2 * N_HEADS, dtype=BF16)
        q2 = jnp.einsum('brhd,hg->bhrgd', q4, eye).reshape(db, 2 * N_HEADS * SAMPLE_ROWS, d)
        new_page = lambda a: jnp.pad(a.reshape(db, SAMPLE_ROWS, d), ((0, 0), (0, page - SAMPLE_ROWS), (0, 0)))
        o = _decode_attention(page_table, q2, new_page(kf).transpose(0, 2, 1), new_page(vf), cache_k, cache_v, l,
                              lamv, subln_g[l], lam_init, dsq)
        x2, h, rinfo = _sample_merge(xs, o.reshape(db * SAMPLE_ROWS, d), bc, sa, sc,
                                     cache_mem_k[l].reshape(db, -1, d), cache_mem_v[l].reshape(db, -1, d),
                                     wts, n_seq=db)
        real = lambda a: a.reshape(db, SAMPLE_ROWS, -1)[:, :dsq].reshape(db * dsq, -1)
        nk_s = real(kf).reshape(db, dsq, 2 * N_HEADS, HEAD_DIM)
        nv_s = real(vf).reshape(db, dsq, N_HEADS, V_DIM)
        nc_s = u.reshape(db, SAMPLE_ROWS, d)[:, dsq - (CONV_WIDTH - 1):dsq]
        xs_moe = (real(x2), real(h).reshape(db * dsq, d // LANES, LANES), real(rinfo))

        outs_p.append((nk_p, nv_p, nc_p, nmk_p, nmv_p))
        outs_s.append((nk_s, nv_s, nc_s))

    y_prompt = _moe_and_final(*xp_moe, w1[depth - 1], w3[depth - 1], w2[depth - 1], ln_final_g).reshape(b, s, d)
    y_sample = _moe_and_final(*xs_moe, w1[depth - 1], w3[depth - 1], w2[depth - 1], ln_final_g).reshape(db, dsq, d)
    stack = lambda items, j: jnp.stack([it[j] for it in items])
    return (y_prompt, y_sample,
            stack(outs_p, 0), stack(outs_p, 1), stack(outs_p, 2), stack(outs_p, 3), stack(outs_p, 4),
            stack(outs_s, 0), stack(outs_s, 1), stack(outs_s, 2))
```

```python
import functools
import math

import jax
import jax.numpy as jnp
from jax import lax
from jax.experimental import pallas as pl
from jax.experimental.pallas import tpu as pltpu

N_HEADS = 8
HEAD_DIM = 64
V_DIM = 2 * HEAD_DIM
ROPE_THETA = 10000.0
CONV_WIDTH = 3
MEM_HEADS = 4
N_GROUPS = 4
EXPERTS_PER_GROUP = 8
N_EXPERTS = N_GROUPS * EXPERTS_PER_GROUP
EXPERT_TOP_K = 2
MOE_BLOCK = 128
EPS = 1e-6
SUBLN_EPS = 1e-5

LANES = 128
SUBLANES = 8
VMEM_LIMIT_BYTES = 56 * 1024 * 1024

MIX_ROWS = 512
ATTN_TILE = 1024
PAGES_PER_STEP = 8
SAMPLE_ROWS = 8
COMBINE_ROWS = 128
MOE_BUFFERS = 3

NEG = -1e30
Q_SCALE = HEAD_DIM ** -0.5 * math.log2(math.e)
F32 = jnp.float32
BF16 = jnp.bfloat16


def _params(sem):
    return pltpu.CompilerParams(dimension_semantics=sem, vmem_limit_bytes=VMEM_LIMIT_BYTES)


def _rms(x, g, eps):
    return x * lax.rsqrt(jnp.mean(x * x, axis=-1, keepdims=True) + eps) * g


def _dot(a, b):
    return jnp.dot(a, b, preferred_element_type=F32)


def _dot_nt(a, b):
    return lax.dot_general(a, b, (((1,), (1,)), ((), ())), preferred_element_type=F32)


def _const_spec(shape):
    nd = len(shape)
    return pl.BlockSpec(shape, lambda *_: (0,) * nd, pipeline_mode=pl.Buffered(1))


def _lambda_value(lamv, lam_init):
    a = jnp.sum(lamv[0:1, :] * lamv[1:2, :], axis=1, keepdims=True)
    b = jnp.sum(lamv[2:3, :] * lamv[3:4, :], axis=1, keepdims=True)
    return jnp.exp(a) - jnp.exp(b) + lam_init


def _head_norm(o, g, lam_init):
    return _rms(o, g, SUBLN_EPS) * (1.0 - lam_init)


def _mixer_in_kernel(*refs, tm, d, seq_tiles, period):
    if period:
        (x_ref, g_ref, w_ref, cos_ref, sin_ref, cw_ref, s1_ref, s2_ref,
         q_ref, kf_ref, kb_ref, vf_ref, vb_ref, bc_ref, sa_ref, sc_ref, tail_ref, ubuf_ref) = refs
    else:
        (x_ref, g_ref, w_ref, cos_ref, sin_ref, cw_ref,
         q_ref, kf_ref, kb_ref, vf_ref, vb_ref, bc_ref, sa_ref, sc_ref, tail_ref, ubuf_ref) = refs
    i = pl.program_id(0)
    h = _rms(x_ref[...], g_ref[...], EPS).astype(BF16)

    def proj(j):
        return _dot(h, w_ref[:, j * d:(j + 1) * d])

    cos = cos_ref[...]
    sin = sin_ref[...]
    lane = lax.broadcasted_iota(jnp.int32, (tm, LANES), 1)
    first_half = (lane % HEAD_DIM) < (HEAD_DIM // 2)

    def rope_chunks(z):
        for c in range(d // LANES):
            zc = z[:, c * LANES:(c + 1) * LANES]
            zr = jnp.where(first_half, pltpu.roll(zc, LANES - HEAD_DIM // 2, 1), pltpu.roll(zc, HEAD_DIM // 2, 1))
            yield c, zc * cos + zr * sin

    zq = proj(0)
    for c, y in rope_chunks(zq):
        q_ref[:, c * LANES:(c + 1) * LANES] = (y * Q_SCALE).astype(BF16)
    zk = proj(1)
    for c, y in rope_chunks(zk):
        if seq_tiles:
            kf_ref[0, c * LANES:(c + 1) * LANES, :] = y.T
        else:
            kf_ref[:, c * LANES:(c + 1) * LANES] = y
        kb_ref[:, c * LANES:(c + 1) * LANES] = y.astype(BF16)
    zv = proj(2)
    vf_ref[...] = zv
    vb_ref[...] = zv.astype(BF16)

    b_gate = proj(3)
    u = proj(4) * proj(5)
    @pl.when(i % max(seq_tiles, 1) == 0)
    def _():
        ubuf_ref[0:SUBLANES, :] = jnp.zeros((SUBLANES, d), F32)
    ubuf_ref[SUBLANES:SUBLANES + tm, :] = u
    p1 = ubuf_ref[SUBLANES - 1:SUBLANES - 1 + tm, :]
    p2 = ubuf_ref[SUBLANES - 2:SUBLANES - 2 + tm, :]
    if period:
        r = lax.broadcasted_iota(jnp.int32, (tm, 1), 0) % period
        p1 = jnp.where(r == 0, s1_ref[...], p1)
        p2 = jnp.where(r < 2, s2_ref[...], p2)
    cw = cw_ref[...]
    y_conv = p2 * cw[0:1, :] + p1 * cw[1:2, :] + u * cw[2:3, :]
    bc_ref[...] = (b_gate * y_conv).astype(BF16)
    if seq_tiles:
        last = ubuf_ref[tm:tm + SUBLANES, :]
        ubuf_ref[0:SUBLANES, :] = last
        tail_ref[0] = last
    else:
        tail_ref[...] = u

    sa_ref[...] = jax.nn.sigmoid(proj(6)).astype(BF16)
    sc_ref[...] = jax.nn.sigmoid(proj(7)).astype(BF16)


def _mixer_in(x2d, ln_g, w_in_b, cos_t, sin_t, conv_w, *, seq_len, state_rows=None):
    t, d = x2d.shape
    tm = min(MIX_ROWS, t)
    assert t % tm == 0
    n_tiles = t // tm
    if state_rows is None:
        assert seq_len % tm == 0
        seq_tiles, period = seq_len // tm, 0
        n_seq = t // seq_len
        pos_tiles = seq_tiles
    else:
        assert tm % seq_len == 0 and n_tiles == 1
        seq_tiles, period = 0, seq_len
        pos_tiles = 1
    row = lambda i: (i, 0)
    pos = lambda i: (i % pos_tiles, 0)
    in_specs = [
        pl.BlockSpec((tm, d), row),
        _const_spec((1, d)),
        _const_spec(w_in_b.shape),
        pl.BlockSpec((tm, LANES), pos),
        pl.BlockSpec((tm, LANES), pos),
        _const_spec((SUBLANES, d)),
    ]
    args = [x2d, ln_g.reshape(1, d), w_in_b, cos_t, sin_t, jnp.pad(conv_w, ((0, SUBLANES - CONV_WIDTH), (0, 0)))]
    bspec = pl.BlockSpec((tm, d), row)
    sd = lambda dt: jax.ShapeDtypeStruct((t, d), dt)
    if period:
        in_specs += [pl.BlockSpec((tm, d), row), pl.BlockSpec((tm, d), row)]
        args += list(state_rows)
        tail_shape = jax.ShapeDtypeStruct((t, d), F32)
        tail_spec = pl.BlockSpec((tm, d), row)
        kf_shape, kf_spec = sd(F32), bspec
    else:
        tail_shape = jax.ShapeDtypeStruct((n_seq, SUBLANES, d), F32)
        tail_spec = pl.BlockSpec((1, SUBLANES, d), lambda i: (i // seq_tiles, 0, 0))
        kf_shape = jax.ShapeDtypeStruct((n_seq, d, seq_len), F32)
        kf_spec = pl.BlockSpec((1, d, tm), lambda i: (i // seq_tiles, 0, i % seq_tiles))
    return pl.pallas_call(
        functools.partial(_mixer_in_kernel, tm=tm, d=d, seq_tiles=seq_tiles, period=period),
        grid=(n_tiles,),
        in_specs=in_specs,
        out_specs=[bspec, kf_spec] + [bspec] * 6 + [tail_spec],
        out_shape=[sd(BF16), kf_shape, sd(BF16), sd(F32), sd(BF16), sd(BF16), sd(BF16), sd(BF16), tail_shape],
        scratch_shapes=[pltpu.VMEM((tm + 2 * SUBLANES, d), F32)],
        compiler_params=_params(("arbitrary",)),
        name="mixer_in",
    )(*args)


def _prompt_attn_kernel(lamv_ref, g_ref, q_ref, k_ref, v_ref, o_ref,
                        s0_ref, s1_ref, mx0_ref, mx1_ref, m_ref, acc_ref, *, tile, lam_init):
    qi = pl.program_id(2)
    q = q_ref[0]
    lane = lax.broadcasted_iota(jnp.int32, q.shape, 1)
    zero = jnp.zeros_like(q)
    qs = (jnp.where(lane < HEAD_DIM, q, zero), jnp.where(lane >= HEAD_DIM, q, zero))
    m_ref[...] = jnp.full(m_ref.shape, NEG, F32)
    acc_ref[...] = jnp.zeros(acc_ref.shape, F32)
    n_chunks = tile // LANES

    s_bufs = (s0_ref, s1_ref)
    mx_bufs = (mx0_ref, mx1_ref)

    def scores(ki, buf, diagonal=False):
        off = pl.multiple_of(ki * tile, tile)
        k = k_ref[0, pl.ds(off, tile), :]
        for j in range(2):
            s = _dot_nt(qs[j], k)
            if diagonal:
                row = lax.broadcasted_iota(jnp.int32, (tile, tile), 0)
                col = lax.broadcasted_iota(jnp.int32, (tile, tile), 1)
                s = jnp.where(col <= row, s, NEG)
            s_bufs[buf][j] = s
            mx_bufs[buf][j] = functools.reduce(
                jnp.maximum, [s[:, c * LANES:(c + 1) * LANES] for c in range(n_chunks)])

    def absorb(ki, buf):
        off = pl.multiple_of(ki * tile, tile)
        v = v_ref[0, pl.ds(off, tile), :]
        v_ones = jnp.concatenate([v, jnp.ones_like(v)], axis=1)
        for j in range(2):
            m_old = m_ref[j]
            m_new = jnp.maximum(m_old, jnp.max(mx_bufs[buf][j], axis=1, keepdims=True))
            alpha = jnp.exp2(m_old - m_new)
            ps = [jnp.exp2(s_bufs[buf][j, :, c * LANES:(c + 1) * LANES] - m_new) for c in range(n_chunks)]
            p = jnp.concatenate(ps, axis=1).astype(BF16)
            acc_ref[j] = jnp.concatenate([alpha, alpha], axis=1) * acc_ref[j] + _dot(p, v_ones)
            m_ref[j] = m_new

    odd = qi % 2 == 1

    @pl.when(qi == 0)
    def _():
        scores(0, 0, diagonal=True)
        absorb(0, 0)

    @pl.when(qi > 0)
    def _():
        scores(0, 0)

    def pair(t, carry):
        i = 2 * t + 1
        scores(i, 1)
        absorb(i - 1, 0)
        scores(i + 1, 0)
        absorb(i, 1)
        return carry

    lax.fori_loop(0, (qi - 1) // 2, pair, 0)

    @pl.when(odd)
    def _():
        scores(qi, 1, diagonal=True)
        absorb(qi - 1, 0)
        absorb(qi, 1)

    @pl.when((qi > 0) & jnp.logical_not(odd))
    def _():
        scores(qi - 1, 1)
        absorb(qi - 2, 0)
        scores(qi, 0, diagonal=True)
        absorb(qi - 1, 1)
        absorb(qi, 0)

    lam = _lambda_value(lamv_ref[...], lam_init)
    o = acc_ref[0, :, :V_DIM] / acc_ref[0, :, V_DIM:] - lam * (acc_ref[1, :, :V_DIM] / acc_ref[1, :, V_DIM:])
    o_ref[0] = _head_norm(o, g_ref[...], lam_init).astype(BF16)


def _prompt_attention(q, k, v, lamv, subln_g, lam_init):
    b, s, d = q.shape
    tile = min(ATTN_TILE, s)
    assert s % tile == 0
    nq = s // tile
    return pl.pallas_call(
        functools.partial(_prompt_attn_kernel, tile=tile, lam_init=lam_init),
        grid=(b, N_HEADS, nq),
        in_specs=[
            _const_spec((SUBLANES, LANES)),
            _const_spec((1, V_DIM)),
            pl.BlockSpec((1, tile, LANES), lambda bi, h, qi: (bi, qi, h)),
            pl.BlockSpec((1, s, LANES), lambda bi, h, qi: (bi, 0, h)),
            pl.BlockSpec((1, s, LANES), lambda bi, h, qi: (bi, 0, h)),
        ],
        out_specs=pl.BlockSpec((1, tile, LANES), lambda bi, h, qi: (bi, qi, h)),
        out_shape=jax.ShapeDtypeStruct((b, s, d), BF16),
        scratch_shapes=[
            pltpu.VMEM((2, tile, tile), F32),
            pltpu.VMEM((2, tile, tile), F32),
            pltpu.VMEM((2, tile, LANES), F32),
            pltpu.VMEM((2, tile, LANES), F32),
            pltpu.VMEM((2, tile, LANES), F32),
            pltpu.VMEM((2, tile, 2 * V_DIM), F32),
        ],
        compiler_params=_params(("arbitrary", "arbitrary", "arbitrary")),
        name="prompt_attention",
    )(lamv, subln_g.reshape(1, V_DIM), q, k, v)


def _decode_attn_kernel(pt_ref, lamv_ref, g_ref, q_ref, kn_ref, vn_ref, *refs, n_pages_step, n_new, lam_init):
    k_refs = refs[:n_pages_step]
    v_refs = refs[n_pages_step:2 * n_pages_step]
    o_ref, m_ref, l_ref, acc_ref = refs[2 * n_pages_step:]
    p_id = pl.program_id(1)
    qbd = q_ref[0]
    n_rows = qbd.shape[0]
    rows_per_vh = 2 * SAMPLE_ROWS

    def update(kt, v, visible=None):
        n_chunks = kt.shape[1] // LANES
        s = _dot(qbd, kt.astype(BF16))
        if visible is not None:
            s = jnp.where(visible, s, NEG)
        chunks = [s[:, c * LANES:(c + 1) * LANES] for c in range(n_chunks)]
        m_old = m_ref[...]
        m_new = jnp.maximum(m_old, jnp.max(functools.reduce(jnp.maximum, chunks), axis=1, keepdims=True))
        alpha = jnp.exp2(m_old - m_new)
        ps = [jnp.exp2(c - m_new) for c in chunks]
        l_ref[...] = alpha * l_ref[...] + functools.reduce(lambda a, b: a + b, ps)
        m_ref[...] = m_new
        p = jnp.concatenate(ps, axis=1).astype(BF16)
        vb = v.astype(BF16)
        for vh in range(N_HEADS):
            rows = slice(vh * rows_per_vh, (vh + 1) * rows_per_vh)
            pv = _dot(p[rows, :], vb[:, vh * V_DIM:(vh + 1) * V_DIM])
            acc_ref[rows, :] = alpha[rows, :] * acc_ref[rows, :] + pv

    @pl.when(p_id == 0)
    def _():
        m_ref[...] = jnp.full(m_ref.shape, NEG, F32)
        l_ref[...] = jnp.zeros(l_ref.shape, F32)
        acc_ref[...] = jnp.zeros(acc_ref.shape, F32)
        shape = (n_rows, kn_ref.shape[2])
        r = lax.broadcasted_iota(jnp.int32, shape, 0) % SAMPLE_ROWS
        j = lax.broadcasted_iota(jnp.int32, shape, 1)
        update(kn_ref[0], vn_ref[0], (j <= r) & (j < n_new))

    d = qbd.shape[1]
    update(jnp.concatenate([r[...].reshape(d, r.shape[2]) for r in k_refs], axis=1),
           jnp.concatenate([pltpu.einshape("phd->p(hd)", r[...]) for r in v_refs], axis=0))

    @pl.when(p_id == pl.num_programs(1) - 1)
    def _():
        lam = _lambda_value(lamv_ref[...], lam_init)
        linv = 1.0 / jnp.sum(l_ref[...], axis=1, keepdims=True)
        g = g_ref[...]
        for vh in range(N_HEADS):
            r1 = slice((2 * vh) * SAMPLE_ROWS, (2 * vh + 1) * SAMPLE_ROWS)
            r2 = slice((2 * vh + 1) * SAMPLE_ROWS, (2 * vh + 2) * SAMPLE_ROWS)
            o1 = acc_ref[r1, :] * linv[r1, :]
            o2 = acc_ref[r2, :] * linv[r2, :]
            o_ref[0, :, vh * V_DIM:(vh + 1) * V_DIM] = _head_norm(o1 - lam * o2, g, lam_init)


def _decode_attention(page_table, q2, k_new, v_new, cache_k, cache_v, layer, lamv, subln_g, lam_init, n_new):
    db, n_rows, d = q2.shape
    n_pages = page_table.shape[1]
    nps = math.gcd(PAGES_PER_STEP, n_pages)
    steps = n_pages // nps
    cache_k = jnp.transpose(cache_k, (0, 1, 3, 4, 2))

    def page_spec(cache, j):
        return pl.BlockSpec((None, None) + cache.shape[2:], lambda b, p, pt: (layer, pt[b, p * nps + j], 0, 0, 0))

    per_seq = lambda a: pl.BlockSpec((1,) + a.shape[1:], lambda b, p, pt: (b,) + (0,) * (a.ndim - 1))
    grid_spec = pltpu.PrefetchScalarGridSpec(
        num_scalar_prefetch=1,
        grid=(db, steps),
        in_specs=[
            pl.BlockSpec((SUBLANES, LANES), lambda b, p, pt: (0, 0)),
            pl.BlockSpec((1, V_DIM), lambda b, p, pt: (0, 0)),
            per_seq(q2), per_seq(k_new), per_seq(v_new),
        ] + [page_spec(cache_k, j) for j in range(nps)] + [page_spec(cache_v, j) for j in range(nps)],
        out_specs=pl.BlockSpec((1, SAMPLE_ROWS, d), lambda b, p, pt: (b, 0, 0)),
        scratch_shapes=[
            pltpu.VMEM((n_rows, LANES), F32),
            pltpu.VMEM((n_rows, LANES), F32),
            pltpu.VMEM((n_rows, V_DIM), F32),
        ],
    )
    return pl.pallas_call(
        functools.partial(_decode_attn_kernel, n_pages_step=nps, n_new=n_new, lam_init=lam_init),
        grid_spec=grid_spec,
        out_shape=jax.ShapeDtypeStruct((db, SAMPLE_ROWS, d), F32),
        compiler_params=_params(("arbitrary", "arbitrary")),
        name="decode_attention",
    )(page_table, lamv, subln_g.reshape(1, V_DIM), q2, k_new, v_new,
      *([cache_k] * nps), *([cache_v] * nps))


def _memory_kv_kernel(mem_ref, g_ref, wk_ref, wv_ref, kf_ref, vf_ref, kb_ref, vb_ref):
    h = _rms(mem_ref[0], g_ref[...], EPS).astype(BF16)
    k = _dot(h, wk_ref[...])
    v = _dot(h, wv_ref[...])
    kf_ref[0] = k
    vf_ref[0] = v
    kb_ref[0] = k.astype(BF16)
    vb_ref[0] = v.astype(BF16)


def _memory_kv(mem, ln_g, w_mk_b, w_mv_b):
    b, m, d = mem.shape
    spec = pl.BlockSpec((1, m, d), lambda i: (i, 0, 0))
    return pl.pallas_call(
        _memory_kv_kernel,
        grid=(b,),
        in_specs=[spec, _const_spec((1, d)), _const_spec((d, d)), _const_spec((d, d))],
        out_specs=[spec] * 4,
        out_shape=[jax.ShapeDtypeStruct((b, m, d), F32)] * 2 + [jax.ShapeDtypeStruct((b, m, d), BF16)] * 2,
        compiler_params=_params(("arbitrary",)),
        name="memory_kv",
    )(mem, ln_g.reshape(1, d), w_mk_b, w_mv_b)


def _merge_vals(x, o, bc, sa, sc, wpa, wpc, wout):
    mixed = sa.astype(F32) * _dot(o.astype(BF16), wpa) + sc.astype(F32) * _dot(bc, wpc)
    return x + _dot(mixed.astype(BF16), wout)


def _memory_attn_vals(qm, mk, mv):
    d = qm.shape[1]
    hd = d // MEM_HEADS
    outs = []
    for h in range(MEM_HEADS):
        sl = slice(h * hd, (h + 1) * hd)
        s = _dot_nt(qm[:, sl].astype(BF16), mk[:, sl]) * (hd ** -0.5)
        e = jnp.exp(s - jnp.max(s, axis=1, keepdims=True))
        p = e / jnp.sum(e, axis=1, keepdims=True)
        outs.append(_dot(p.astype(BF16), mv[:, sl]))
    return jnp.concatenate(outs, axis=1)


def _router_vals(h, wr_hi, wr_lo, br):
    h_hi = h.astype(BF16)
    h_lo = (h - h_hi.astype(F32)).astype(BF16)
    lg = _dot(h_hi, wr_hi) + (_dot(h_lo, wr_hi) + _dot(h_hi, wr_lo)) + br
    lane = lax.broadcasted_iota(jnp.int32, lg.shape, 1)
    big = jnp.int32(4 * LANES)
    is_g = (lane >= N_EXPERTS) & (lane < N_EXPERTS + N_GROUPS)
    gl = jnp.where(is_g, lg, NEG)
    gmax = jnp.max(gl, axis=1, keepdims=True)
    gidx = jnp.min(jnp.where(is_g & (gl == gmax), lane - N_EXPERTS, big), axis=1, keepdims=True)
    gprob = 1.0 / jnp.sum(jnp.where(is_g, jnp.exp(gl - gmax), 0.0), axis=1, keepdims=True)
    in_group = (lane < N_EXPERTS) & ((lane // EXPERTS_PER_GROUP) == gidx)
    el = jnp.where(in_group, lg, NEG)
    l1 = jnp.max(el, axis=1, keepdims=True)
    i1 = jnp.min(jnp.where(in_group & (el == l1), lane, big), axis=1, keepdims=True)
    rest = in_group & (lane != i1)
    el2 = jnp.where(rest, lg, NEG)
    l2 = jnp.max(el2, axis=1, keepdims=True)
    i2 = jnp.min(jnp.where(rest & (el2 == l2), lane, big), axis=1, keepdims=True)
    e21 = jnp.exp(l2 - l1)
    p1 = 1.0 / (1.0 + e21)
    g1 = gprob * p1
    g2 = gprob * (e21 * p1)
    return jnp.where(lane == 0, i1.astype(F32),
                     jnp.where(lane == 1, i2.astype(F32),
                               jnp.where(lane == 2, g1, jnp.where(lane == 3, g2, 0.0))))


def _store_token_tiles(ref, x):
    for c in range(ref.shape[1]):
        ref[:, c, :] = x[:, c * LANES:(c + 1) * LANES]


def _post_vals(x1, om, wmo, gffn, wr_hi, wr_lo, br):
    x2 = x1 + _dot(om.astype(BF16), wmo)
    h = _rms(x2, gffn, EPS)
    return x2, h, _router_vals(h, wr_hi, wr_lo, br)


def _prompt_merge_kernel(x_ref, o_ref, bc_ref, sa_ref, sc_ref, mk_ref, mv_ref,
                         wpa_ref, wpc_ref, wout_ref, gmem_ref, wmq_ref, wmo_ref,
                         gffn_ref, wrh_ref, wrl_ref, br_ref,
                         x2_ref, h_ref, ri_ref):
    x1 = _merge_vals(x_ref[...], o_ref[...], bc_ref[...], sa_ref[...], sc_ref[...],
                     wpa_ref[...], wpc_ref[...], wout_ref[...])
    qm = _dot(_rms(x1, gmem_ref[...], EPS).astype(BF16), wmq_ref[...])
    om = _memory_attn_vals(qm, mk_ref[0], mv_ref[0])
    x2, h, ri = _post_vals(x1, om, wmo_ref[...], gffn_ref[...], wrh_ref[...], wrl_ref[...], br_ref[...])
    x2_ref[...] = x2
    _store_token_tiles(h_ref, h)
    ri_ref[...] = ri


def _prompt_merge(x2d, o, bc, sa, sc, mk_b, mv_b, wts, *, seq_len):
    t, d = x2d.shape
    tm = min(MIX_ROWS, seq_len)
    assert seq_len % tm == 0
    seq_tiles = seq_len // tm
    n_mem = mk_b.shape[1]
    row = pl.BlockSpec((tm, d), lambda i: (i, 0))
    mem = pl.BlockSpec((1, n_mem, d), lambda i: (i // seq_tiles, 0, 0))
    return pl.pallas_call(
        _prompt_merge_kernel,
        grid=(t // tm,),
        in_specs=[row] * 5 + [mem] * 2 + [_const_spec(w.shape) for w in wts],
        out_specs=[row, pl.BlockSpec((tm, d // LANES, LANES), lambda i: (i, 0, 0)),
                   pl.BlockSpec((tm, LANES), lambda i: (i, 0))],
        out_shape=[jax.ShapeDtypeStruct((t, d), F32), jax.ShapeDtypeStruct((t, d // LANES, LANES), F32),
                   jax.ShapeDtypeStruct((t, LANES), F32)],
        compiler_params=_params(("arbitrary",)),
        name="prompt_merge",
    )(x2d, o, bc, sa, sc, mk_b, mv_b, *wts)


def _sample_merge_kernel(x_ref, o_ref, bc_ref, sa_ref, sc_ref, wpa_ref, wpc_ref, wout_ref, gmem_ref, wmq_ref,
                         x1_ref, qm_ref):
    x1 = _merge_vals(x_ref[...], o_ref[...], bc_ref[...], sa_ref[...], sc_ref[...],
                     wpa_ref[...], wpc_ref[...], wout_ref[...])
    x1_ref[...] = x1
    qm_ref[...] = _dot(_rms(x1, gmem_ref[...], EPS).astype(BF16), wmq_ref[...])


def _sample_memattn_kernel(qm_ref, mk_ref, mv_ref, om_ref):
    flat = lambda ref: pltpu.einshape("mhd->m(hd)", ref[0]).astype(BF16)
    om_ref[0] = _memory_attn_vals(qm_ref[0], flat(mk_ref), flat(mv_ref))


def _sample_post_kernel(x1_ref, om_ref, wmo_ref, gffn_ref, wrh_ref, wrl_ref, br_ref, x2_ref, h_ref, ri_ref):
    x2, h, ri = _post_vals(x1_ref[...], om_ref[...], wmo_ref[...], gffn_ref[...],
                           wrh_ref[...], wrl_ref[...], br_ref[...])
    x2_ref[...] = x2
    _store_token_tiles(h_ref, h)
    ri_ref[...] = ri


def _sample_merge(x2d, o, bc, sa, sc, mem_k, mem_v, wts, *, n_seq):
    t, d = x2d.shape
    wpa, wpc, wout, gmem, wmq, wmo, gffn, wrh, wrl, br = wts
    full = pl.BlockSpec((t, d), lambda i: (0, 0))
    x1, qm = pl.pallas_call(
        _sample_merge_kernel,
        grid=(1,),
        in_specs=[full] * 5 + [_const_spec(w.shape) for w in (wpa, wpc, wout, gmem, wmq)],
        out_specs=[full, full],
        out_shape=[jax.ShapeDtypeStruct((t, d), F32)] * 2,
        compiler_params=_params(("arbitrary",)),
        name="sample_merge",
    )(x2d, o, bc, sa, sc, wpa, wpc, wout, gmem, wmq)
    rows = t // n_seq
    seq = pl.BlockSpec((1, rows, d), lambda i: (i, 0, 0))
    mem = pl.BlockSpec((1,) + mem_k.shape[1:], lambda i: (i, 0, 0, 0))
    om = pl.pallas_call(
        _sample_memattn_kernel,
        grid=(n_seq,),
        in_specs=[seq, mem, mem],
        out_specs=seq,
        out_shape=jax.ShapeDtypeStruct((n_seq, rows, d), F32),
        compiler_params=_params(("arbitrary",)),
        name="sample_memattn",
    )(qm.reshape(n_seq, rows, d), mem_k, mem_v)
    return pl.pallas_call(
        _sample_post_kernel,
        grid=(1,),
        in_specs=[full, full] + [_const_spec(w.shape) for w in (wmo, gffn, wrh, wrl, br)],
        out_specs=[full, pl.BlockSpec((t, d // LANES, LANES), lambda i: (0, 0, 0)),
                   pl.BlockSpec((t, LANES), lambda i: (0, 0))],
        out_shape=[jax.ShapeDtypeStruct((t, d), F32), jax.ShapeDtypeStruct((t, d // LANES, LANES), F32),
                   jax.ShapeDtypeStruct((t, LANES), F32)],
        compiler_params=_params(("arbitrary",)),
        name="sample_post",
    )(x1, om.reshape(t, d), wmo, gffn, wrh, wrl, br)


def _row_copy(src_hbm, idx, dst, r, sem):
    return pltpu.make_async_copy(src_hbm.at[pl.ds(idx, 1), :], dst.at[pl.ds(r, 1), :], sem)


def _token_copy(src_hbm, idx, dst, r, sem):
    return pltpu.make_async_copy(src_hbm.at[idx], dst.at[r], sem)


def _moe_kernel(blk_e_ref, tok0_ref, tok1_ref, tok_next_ref, h_hbm, w1_ref, w3_ref, w2_ref, ys_ref,
                xbuf, w1b_ref, w3b_ref, w2b_ref, sem):
    i = pl.program_id(0)
    last = pl.num_programs(0) - 1
    cur, nxt = i % MOE_BUFFERS, (i + MOE_BUFFERS - 1) % MOE_BUFFERS
    d, de = w1_ref.shape[1], w1_ref.shape[2]

    def wait_rows(buf):
        for r in range(MOE_BLOCK):
            _token_copy(h_hbm, 0, xbuf.at[buf], r, sem.at[buf]).wait()

    @pl.when(i == 0)
    def _():
        for b, toks in enumerate((tok0_ref, tok1_ref)):
            for r in range(MOE_BLOCK):
                _token_copy(h_hbm, toks[0, 0, r], xbuf.at[b], r, sem.at[b]).start()

    @pl.when((i == 0) | (blk_e_ref[i] != blk_e_ref[jnp.maximum(i - 1, 0)]))
    def _():
        w1b_ref[...] = w1_ref[0].astype(BF16)
        w3b_ref[...] = w3_ref[0].astype(BF16)
        w2b_ref[...] = w2_ref[0].astype(BF16)

    wait_rows(cur)
    x = pltpu.einshape("tcl->t(cl)", xbuf[cur]).astype(BF16)
    rows = iter(range(MOE_BLOCK))
    n_chunks = 4
    per_chunk = MOE_BLOCK // (3 * n_chunks)

    def start_rows(n):
        for _ in range(n):
            r = next(rows)
            _token_copy(h_hbm, tok_next_ref[0, 0, r], xbuf.at[nxt], r, sem.at[nxt]).start(priority=r % 2)

    def up(w_ref):
        acc = None
        for c in range(n_chunks):
            start_rows(per_chunk)
            ks = slice(c * (d // n_chunks), (c + 1) * (d // n_chunks))
            part = _dot(x[:, ks], w_ref[ks, :])
            acc = part if acc is None else acc + part
        return acc

    a = up(w1b_ref)
    b = up(w3b_ref)
    hmid = (jax.nn.silu(a) * b).astype(BF16)
    for c in range(n_chunks):
        start_rows(per_chunk)
        ns = slice(c * (d // n_chunks), (c + 1) * (d // n_chunks))
        ys_ref[:, ns] = _dot(hmid, w2b_ref[:, ns])
    start_rows(MOE_BLOCK - 3 * n_chunks * per_chunk)

    @pl.when(i == last)
    def _():
        for b in range(1, MOE_BUFFERS):
            wait_rows((i + b) % MOE_BUFFERS)


def _moe_experts(h, slot_tok, blk_e, w1, w3, w2):
    d = h.shape[1] * h.shape[2]
    n_blocks = blk_e.shape[0]
    de = w1.shape[2]
    toks = slot_tok.reshape(n_blocks, 1, MOE_BLOCK)
    tok_spec = lambda f: pl.BlockSpec((1, 1, MOE_BLOCK), f, memory_space=pltpu.SMEM)
    grid_spec = pltpu.PrefetchScalarGridSpec(
        num_scalar_prefetch=1,
        grid=(n_blocks,),
        in_specs=[
            tok_spec(lambda i, be: (i, 0, 0)),
            tok_spec(lambda i, be: (jnp.minimum(i + 1, n_blocks - 1), 0, 0)),
            tok_spec(lambda i, be: (jnp.minimum(i + MOE_BUFFERS - 1, n_blocks - 1), 0, 0)),
            pl.BlockSpec(memory_space=pl.ANY),
            pl.BlockSpec((1, d, de), lambda i, be: (be[i], 0, 0)),
            pl.BlockSpec((1, d, de), lambda i, be: (be[i], 0, 0)),
            pl.BlockSpec((1, de, d), lambda i, be: (be[i], 0, 0)),
        ],
        out_specs=pl.BlockSpec((MOE_BLOCK, d), lambda i, be: (i, 0)),
        scratch_shapes=[
            pltpu.VMEM((MOE_BUFFERS, MOE_BLOCK, d // LANES, LANES), F32),
            pltpu.VMEM((d, de), BF16), pltpu.VMEM((d, de), BF16), pltpu.VMEM((de, d), BF16),
            pltpu.SemaphoreType.DMA((MOE_BUFFERS,)),
        ],
    )
    return pl.pallas_call(
        _moe_kernel,
        grid_spec=grid_spec,
        out_shape=jax.ShapeDtypeStruct((n_blocks * MOE_BLOCK, d), F32),
        compiler_params=_params(("arbitrary",)),
        name="moe_experts",
    )(blk_e, toks, toks, toks, h, w1, w3, w2)


def _combine_kernel(slot_ref, slot_next_ref, x2_ref, ri_ref, g_ref, ys_hbm, y_ref, ybuf, sem, *, tm):
    i = pl.program_id(0)
    n = pl.num_programs(0)

    def gather(slots, buf):
        for r in range(2 * tm):
            _row_copy(ys_hbm, slots[0, 0, r], ybuf.at[buf], r, sem.at[buf]).start()

    @pl.when(i == 0)
    def _():
        gather(slot_ref, 0)

    @pl.when(i + 1 < n)
    def _():
        gather(slot_next_ref, (i + 1) % 2)

    buf = i % 2
    for r in range(2 * tm):
        _row_copy(ys_hbm, 0, ybuf.at[buf], r, sem.at[buf]).wait()
    ri = ri_ref[...]
    y = x2_ref[...] + ri[:, 2:3] * ybuf[buf, 0:tm, :] + ri[:, 3:4] * ybuf[buf, tm:2 * tm, :]
    y_ref[...] = _rms(y, g_ref[...], EPS)


def _moe_combine(x2, rinfo, slots, ys, ln_g):
    t, d = x2.shape
    tm = min(COMBINE_ROWS, t)
    assert t % tm == 0
    n = t // tm
    sl = slots.reshape(n, tm, 2).transpose(0, 2, 1).reshape(n, 1, 2 * tm)
    slot_spec = lambda f: pl.BlockSpec((1, 1, 2 * tm), f, memory_space=pltpu.SMEM)
    row = pl.BlockSpec((tm, d), lambda i: (i, 0))
    return pl.pallas_call(
        functools.partial(_combine_kernel, tm=tm),
        grid=(n,),
        in_specs=[
            slot_spec(lambda i: (i, 0, 0)),
            slot_spec(lambda i: (jnp.minimum(i + 1, n - 1), 0, 0)),
            row,
            pl.BlockSpec((tm, LANES), lambda i: (i, 0)),
            _const_spec((1, d)),
            pl.BlockSpec(memory_space=pl.ANY),
        ],
        out_specs=row,
        out_shape=jax.ShapeDtypeStruct((t, d), F32),
        scratch_shapes=[pltpu.VMEM((2, 2 * tm, d), F32), pltpu.SemaphoreType.DMA((2,))],
        compiler_params=_params(("arbitrary",)),
        name="moe_combine",
    )(sl, sl, x2, rinfo, ln_g.reshape(1, d), ys)


def _moe_plan(rinfo):
    t = rinfo.shape[0]
    eid = rinfo[:, :EXPERT_TOP_K].astype(jnp.int32).reshape(-1)
    n_assign = t * EXPERT_TOP_K
    experts = jnp.arange(N_EXPERTS, dtype=jnp.int32)
    order = jnp.argsort(eid, stable=True)
    pos = jnp.argsort(order)
    counts = jnp.sum((eid[:, None] == experts[None, :]).astype(jnp.int32), axis=0)
    starts = jnp.cumsum(counts) - counts
    padded = (counts + MOE_BLOCK - 1) // MOE_BLOCK * MOE_BLOCK
    pends = jnp.cumsum(padded)
    pstarts = pends - padded
    slots = (pos + (pstarts - starts)[eid]).reshape(t, EXPERT_TOP_K)
    n_blocks = -(-n_assign // MOE_BLOCK) + N_EXPERTS
    blk_start = jnp.arange(n_blocks, dtype=jnp.int32) * MOE_BLOCK
    blk_e = jnp.minimum(jnp.sum((pends[None, :] <= blk_start[:, None]).astype(jnp.int32), axis=1), N_EXPERTS - 1)
    j = (blk_start - pstarts[blk_e])[:, None] + jnp.arange(MOE_BLOCK, dtype=jnp.int32)[None, :]
    valid = (j >= 0) & (j < counts[blk_e][:, None])
    src = jnp.clip(starts[blk_e][:, None] + j, 0, n_assign - 1)
    filler = (blk_start[:, None] + jnp.arange(MOE_BLOCK, dtype=jnp.int32)[None, :]) % t
    slot_tok = jnp.where(valid, order[src] // EXPERT_TOP_K, filler).astype(jnp.int32).reshape(-1)
    return slots.astype(jnp.int32), slot_tok, blk_e.astype(jnp.int32)


def _moe_and_final(x2, h, rinfo, w1, w3, w2, ln_final_g):
    slots, slot_tok, blk_e = _moe_plan(rinfo)
    ys = _moe_experts(h, slot_tok, blk_e, w1, w3, w2)
    return _moe_combine(x2, rinfo, slots, ys, ln_final_g)


def _rope_tables(pos):
    half = HEAD_DIM // 2
    inv = 1.0 / (ROPE_THETA ** (jnp.arange(half, dtype=F32) / half))
    ang = pos.astype(F32)[:, None] * inv[None, :]
    cos = jnp.cos(ang)
    sin = jnp.sin(ang)
    cos_t = jnp.tile(jnp.concatenate([cos, cos], axis=1), (1, LANES // HEAD_DIM))
    sin_t = jnp.tile(jnp.concatenate([-sin, sin], axis=1), (1, LANES // HEAD_DIM))
    return cos_t, sin_t


def _lambda_init(layer_idx):
    return 0.8 - 0.6 * math.exp(-0.3 * layer_idx)


def _split_hi_lo(w):
    hi = w.astype(BF16)
    return hi, (w - hi.astype(F32)).astype(BF16)


def kernel(x_prompt, x_sample, cache_k, cache_v, state_conv, cache_mem_k, cache_mem_v, page_table, mem_prompt, ln_mix_g, w_in, lambda_q1, lambda_k1, lambda_q2, lambda_k2, subln_g, conv_w, w_proj_attn, w_proj_conv, w_out, ln_mem_g, ln_memkv_g, w_mq, w_mk, w_mv, w_mo, ln_ffn_g, w_group, b_group, w_router, b_router, w1, w3, w2, ln_final_g):
    b, s, d = x_prompt.shape
    db, dsq, _ = x_sample.shape
    depth = w_in.shape[0]
    n_pool, page = cache_k.shape[1], cache_k.shape[2]
    past = page_table.shape[1] * page
    assert d == 2 * N_HEADS * HEAD_DIM and w_in.shape[2] == 8 * d
    assert depth == 1, "only the last layer feeds the final norm; deeper stacks are not supported"
    assert CONV_WIDTH - 1 <= dsq <= SAMPLE_ROWS <= page
    pad = SAMPLE_ROWS - dsq

    cos_p, sin_p = _rope_tables(jnp.arange(s, dtype=jnp.int32))
    pos_s = past + jnp.minimum(jnp.arange(SAMPLE_ROWS, dtype=jnp.int32), dsq - 1)
    cos_s, sin_s = (jnp.tile(a, (db, 1)) for a in _rope_tables(pos_s))

    xp = x_prompt.reshape(b * s, d)
    xs = jnp.pad(x_sample, ((0, 0), (0, pad), (0, 0))).reshape(db * SAMPLE_ROWS, d)
    outs_p, outs_s = [], []
    for l in range(depth):
        lam_init = _lambda_init(l)
        lamv = jnp.pad(jnp.stack([lambda_q1[l], lambda_k1[l], lambda_q2[l], lambda_k2[l]]),
                       ((0, SUBLANES - 4), (0, LANES - HEAD_DIM)))
        w_in_b = w_in[l].astype(BF16)
        wr = jnp.pad(jnp.concatenate([w_router[l], w_group[l]], axis=1), ((0, 0), (0, LANES - N_EXPERTS - N_GROUPS)))
        br = jnp.pad(jnp.concatenate([b_router[l], b_group[l]]), (0, LANES - N_EXPERTS - N_GROUPS)).reshape(1, LANES)
        wr_hi, wr_lo = _split_hi_lo(wr)
        wts = (w_proj_attn[l].astype(BF16), w_proj_conv[l].astype(BF16), w_out[l].astype(BF16),
               ln_mem_g[l].reshape(1, d), w_mq[l].astype(BF16), w_mo[l].astype(BF16),
               ln_ffn_g[l].reshape(1, d), wr_hi, wr_lo, br)

        q, kf, kb, vf, vb, bc, sa, sc, tail = _mixer_in(xp, ln_mix_g[l], w_in_b, cos_p, sin_p, conv_w[l], seq_len=s)
        o = _prompt_attention(q.reshape(b, s, d), kb.reshape(b, s, d), vb.reshape(b, s, d), lamv, subln_g[l], lam_init)
        mkf, mvf, mkb, mvb = _memory_kv(mem_prompt, ln_memkv_g[l], w_mk[l].astype(BF16), w_mv[l].astype(BF16))
        x2, h, rinfo = _prompt_merge(xp, o.reshape(b * s, d), bc, sa, sc, mkb, mvb, wts, seq_len=s)
        nk_p = kf.reshape(b, 2 * N_HEADS, HEAD_DIM, s).transpose(0, 3, 1, 2)
        nv_p = vf.reshape(b, s, N_HEADS, V_DIM)
        nc_p = tail[:, SUBLANES - (CONV_WIDTH - 1):, :]
        nmk_p = mkf.reshape(b, -1, MEM_HEADS, d // MEM_HEADS)
        nmv_p = mvf.reshape(b, -1, MEM_HEADS, d // MEM_HEADS)
        xp_moe = (x2, h, rinfo)

        st = state_conv[l]
        zrow = jnp.zeros((db, SAMPLE_ROWS - 1, d), F32)
        s1 = jnp.concatenate([st[:, 1:2], zrow], axis=1).reshape(db * SAMPLE_ROWS, d)
        s2 = jnp.concatenate([st, zrow[:, 1:]], axis=1).reshape(db * SAMPLE_ROWS, d)
        q, kf, kb, vf, vb, bc, sa, sc, u = _mixer_in(xs, ln_mix_g[l], w_in_b, cos_s, sin_s, conv_w[l],
                                                      seq_len=SAMPLE_ROWS, state_rows=(s1, s2))
        q4 = q.reshape(db, SAMPLE_ROWS, 2 * N_HEADS, HEAD_DIM)
        eye = jnp.eye(2 * N_HEADS, dtype=BF16)
        q2 = jnp.einsum('brhd,hg->bhrgd', q4, eye).reshape(db, 2 * N_HEADS * SAMPLE_ROWS, d)
        new_page = lambda a: jnp.pad(a.reshape(db, SAMPLE_ROWS, d), ((0, 0), (0, page - SAMPLE_ROWS), (0, 0)))
        o = _decode_attention(page_table, q2, new_page(kf).transpose(0, 2, 1), new_page(vf), cache_k, cache_v, l,
                              lamv, subln_g[l], lam_init, dsq)
        x2, h, rinfo = _sample_merge(xs, o.reshape(db * SAMPLE_ROWS, d), bc, sa, sc,
                                     cache_mem_k[l], cache_mem_v[l],
                                     wts, n_seq=db)
        real = lambda a: a.reshape(db, SAMPLE_ROWS, -1)[:, :dsq].reshape(db * dsq, -1)
        nk_s = real(kf).reshape(db, dsq, 2 * N_HEADS, HEAD_DIM)
        nv_s = real(vf).reshape(db, dsq, N_HEADS, V_DIM)
        nc_s = u.reshape(db, SAMPLE_ROWS, d)[:, dsq - (CONV_WIDTH - 1):dsq]
        xs_moe = (real(x2), real(h).reshape(db * dsq, d // LANES, LANES), real(rinfo))

        outs_p.append((nk_p, nv_p, nc_p, nmk_p, nmv_p))
        outs_s.append((nk_s, nv_s, nc_s))

    y_prompt = _moe_and_final(*xp_moe, w1[depth - 1], w3[depth - 1], w2[depth - 1], ln_final_g).reshape(b, s, d)
    y_sample = _moe_and_final(*xs_moe, w1[depth - 1], w3[depth - 1], w2[depth - 1], ln_final_g).reshape(db, dsq, d)
    stack = lambda items, j: jnp.stack([it[j] for it in items])
    return (y_prompt, y_sample,
            stack(outs_p, 0), stack(outs_p, 1), stack(outs_p, 2), stack(outs_p, 3), stack(outs_p, 4),
            stack(outs_s, 0), stack(outs_s, 1), stack(outs_s, 2))
```
